```python
import math
import jax, jax.numpy as jnp
from jax import lax
import numpy as np

D_MODEL = 1024
BATCH = 4
SEQ = 8192
DEPTH = 4

N_MIXERS = 3
GRID_W = 64
Q_BLOCK = 128
EPS = 1e-6
D_FF = 2816
FFN_RES = 0.5

DIFF_HEAD_DIM = 64
DIFF_HEADS = D_MODEL // (2 * DIFF_HEAD_DIM)
ROPE_THETA = 500000.0
ROPE_DIMS = DIFF_HEAD_DIM // 4

NA_HEAD_DIM = 64
NA_HEADS = D_MODEL // NA_HEAD_DIM
NA_WIN_H = 8
NA_WIN_W = 16

GQA_HEAD_DIM = 64
GQA_Q_HEADS = D_MODEL // GQA_HEAD_DIM
GQA_KV_HEADS = GQA_Q_HEADS // 4
AXIAL_THETA = 10000.0

N_A = len(range(0, DEPTH, N_MIXERS))
N_B = len(range(1, DEPTH, N_MIXERS))
N_C = len(range(2, DEPTH, N_MIXERS))

kernel_name = 'hybrid_diff_na_gqa_macaron_encoder'


def _rmsnorm(x, g):
    xf = x.astype(jnp.float32)
    y = xf * lax.rsqrt(jnp.mean(xf * xf, axis=-1, keepdims=True) + EPS)
    return (y * g.astype(jnp.float32)).astype(x.dtype)


def _swiglu(x, wg, wu, wd):
    return (jax.nn.silu(x @ wg) * (x @ wu)) @ wd


def _rope_angles(pos, dims, theta):
    inv = theta ** (-jnp.arange(0, dims, 2, dtype=jnp.float32) / dims)
    return pos[:, None] * inv[None, :]


def _rotate(x, ang):
    m = ang.shape[1]
    ang = ang.reshape((1, ang.shape[0]) + (1,) * (x.ndim - 3) + (m,))
    cos = jnp.cos(ang).astype(x.dtype)
    sin = jnp.sin(ang).astype(x.dtype)
    x1, x2 = x[..., :m], x[..., m:]
    return jnp.concatenate([x1 * cos - x2 * sin, x2 * cos + x1 * sin], axis=-1)


def _to_blocks(q):
    b, s = q.shape[:2]
    return jnp.moveaxis(q.reshape((b, s // Q_BLOCK, Q_BLOCK) + q.shape[2:]), 1, 0)


def _from_blocks(o):
    o = jnp.moveaxis(o, 0, 1)
    return o.reshape((o.shape[0], o.shape[1] * o.shape[2]) + o.shape[3:])


def _diff_attention(h, w_in, w_out, lam_vecs, subln_g, layer_idx):
    b, s, _ = h.shape
    nh, dh = DIFF_HEADS, DIFF_HEAD_DIM
    q, k, v = jnp.split(h @ w_in, 3, axis=-1)
    q = q.reshape(b, s, nh, 2, dh)
    k = k.reshape(b, s, nh, 2, dh)
    v = v.reshape(b, s, nh, 2 * dh)
    ang = _rope_angles(jnp.arange(s, dtype=jnp.float32), ROPE_DIMS, ROPE_THETA)
    q = jnp.concatenate([_rotate(q[..., :ROPE_DIMS], ang), q[..., ROPE_DIMS:]], axis=-1)
    k = jnp.concatenate([_rotate(k[..., :ROPE_DIMS], ang), k[..., ROPE_DIMS:]], axis=-1)
    lam_init = 0.8 - 0.6 * math.exp(-0.3 * layer_idx)
    lv = lam_vecs.astype(jnp.float32)
    lam = jnp.exp(jnp.sum(lv[0] * lv[1])) - jnp.exp(jnp.sum(lv[2] * lv[3])) + lam_init
    scale = dh ** -0.5

    def block(qb):
        sc = jnp.einsum('bqhcd,bshcd->bhcqs', qb, k).astype(jnp.float32) * scale
        p = jax.nn.softmax(sc, axis=-1)
        a = (p[:, :, 0] - lam * p[:, :, 1]).astype(v.dtype)
        return jnp.einsum('bhqs,bshe->bqhe', a, v)

    o = _from_blocks(lax.map(block, _to_blocks(q)))
    o = _rmsnorm(o, subln_g) * (1.0 - lam_init)
    return o.reshape(b, s, nh * 2 * dh) @ w_out


def _neighborhood_attention(h, w_in, w_out, rpb):
    b, s, _ = h.shape
    nh, dh = NA_HEADS, NA_HEAD_DIM
    rows = s // GRID_W
    kh = min(NA_WIN_H, rows)
    kw = NA_WIN_W
    q, k, v = jnp.split(h @ w_in, 3, axis=-1)
    q = q.reshape(b, rows, GRID_W, nh, dh)
    k = k.reshape(b, rows, GRID_W, nh, dh)
    v = v.reshape(b, rows, GRID_W, nh, dh)
    cols = jnp.arange(GRID_W)
    col_start = jnp.clip(cols - kw // 2, 0, GRID_W - kw)
    col_idx = col_start[:, None] + jnp.arange(kw)[None, :]
    dc = col_idx - cols[:, None] + (NA_WIN_W - 1)
    scale = dh ** -0.5

    def row_step(r):
        r0 = jnp.clip(r - kh // 2, 0, rows - kh)
        dr = r0 + jnp.arange(kh) - r + (NA_WIN_H - 1)
        k_band = lax.dynamic_slice_in_dim(k, r0, kh, axis=1)
        v_band = lax.dynamic_slice_in_dim(v, r0, kh, axis=1)
        k_win = k_band[:, :, col_idx]
        v_win = v_band[:, :, col_idx]
        q_row = lax.dynamic_index_in_dim(q, r, axis=1, keepdims=False)
        bias = rpb[:, dr[None, :, None], dc[:, None, :]]
        sc = jnp.einsum('bqhd,brqjhd->bhqrj', q_row, k_win).astype(jnp.float32) * scale
        sc = sc + bias.astype(jnp.float32)[None]
        p = jax.nn.softmax(sc.reshape(b, nh, GRID_W, kh * kw), axis=-1)
        p = p.reshape(sc.shape).astype(v.dtype)
        return jnp.einsum('bhqrj,brqjhd->bqhd', p, v_win)

    o = lax.map(row_step, jnp.arange(rows))
    o = jnp.moveaxis(o, 0, 1).reshape(b, s, nh * dh)
    return o @ w_out


def _gqa_axial(h, w_in, w_out, qk_g):
    b, s, _ = h.shape
    nq, nkv, dh = GQA_Q_HEADS, GQA_KV_HEADS, GQA_HEAD_DIM
    grp = nq // nkv
    qkv = h @ w_in
    q = qkv[..., :nq * dh].reshape(b, s, nkv, grp, dh)
    k = qkv[..., nq * dh:(nq + nkv) * dh].reshape(b, s, nkv, dh)
    v = qkv[..., (nq + nkv) * dh:].reshape(b, s, nkv, dh)
    q = _rmsnorm(q, qk_g[0])
    k = _rmsnorm(k, qk_g[1])
    t = jnp.arange(s)
    half = dh // 2
    ang_r = _rope_angles((t // GRID_W).astype(jnp.float32), half, AXIAL_THETA)
    ang_c = _rope_angles((t % GRID_W).astype(jnp.float32), half, AXIAL_THETA)
    q = jnp.concatenate([_rotate(q[..., :half], ang_r), _rotate(q[..., half:], ang_c)], axis=-1)
    k = jnp.concatenate([_rotate(k[..., :half], ang_r), _rotate(k[..., half:], ang_c)], axis=-1)
    scale = dh ** -0.5

    def block(qb):
        sc = jnp.einsum('bqkgd,bskd->bkgqs', qb, k).astype(jnp.float32) * scale
        p = jax.nn.softmax(sc, axis=-1).astype(v.dtype)
        return jnp.einsum('bkgqs,bskd->bqkgd', p, v)

    o = _from_blocks(lax.map(block, _to_blocks(q)))
    return o.reshape(b, s, nq * dh) @ w_out


def setup_inputs(seed: int = 0) -> dict:
    key = jax.random.key(seed)
    ks = jax.random.split(key, 20)

    def w(k, shape, fan_in):
        return jax.random.normal(k, shape, jnp.float32) * fan_in ** -0.5

    def gain(k, shape):
        return 1.0 + 0.05 * jax.random.normal(k, shape, jnp.float32)

    diff_w = DIFF_HEADS * 2 * DIFF_HEAD_DIM
    na_w = NA_HEADS * NA_HEAD_DIM
    gqa_q = GQA_Q_HEADS * GQA_HEAD_DIM
    gqa_in = (GQA_Q_HEADS + 2 * GQA_KV_HEADS) * GQA_HEAD_DIM
    return {
        'x': jax.random.normal(ks[0], (BATCH, SEQ, D_MODEL), jnp.float32),
        'norm_g': gain(ks[1], (DEPTH, 6, D_MODEL)),
        'ffn1_wg': w(ks[2], (DEPTH, D_MODEL, D_FF), D_MODEL),
        'ffn1_wu': w(ks[3], (DEPTH, D_MODEL, D_FF), D_MODEL),
        'ffn1_wd': w(ks[4], (DEPTH, D_FF, D_MODEL), D_FF),
        'ffn2_wg': w(ks[5], (DEPTH, D_MODEL, D_FF), D_MODEL),
        'ffn2_wu': w(ks[6], (DEPTH, D_MODEL, D_FF), D_MODEL),
        'ffn2_wd': w(ks[7], (DEPTH, D_FF, D_MODEL), D_FF),
        'diff_w_in': w(ks[8], (N_A, D_MODEL, 3 * diff_w), D_MODEL),
        'diff_w_out': w(ks[9], (N_A, diff_w, D_MODEL), diff_w),
        'diff_lambda': 0.1 * jax.random.normal(ks[10], (N_A, 4, DIFF_HEAD_DIM), jnp.float32),
        'diff_subln': gain(ks[11], (N_A, 2 * DIFF_HEAD_DIM)),
        'na_w_in': w(ks[12], (N_B, D_MODEL, 3 * na_w), D_MODEL),
        'na_w_out': w(ks[13], (N_B, na_w, D_MODEL), na_w),
        'na_rpb': 0.1 * jax.random.normal(ks[14], (N_B, NA_HEADS, 2 * NA_WIN_H - 1, 2 * NA_WIN_W - 1), jnp.float32),
        'gqa_w_in': w(ks[15], (N_C, D_MODEL, gqa_in), D_MODEL),
        'gqa_w_out': w(ks[16], (N_C, gqa_q, D_MODEL), gqa_q),
        'gqa_qk_norm': gain(ks[17], (N_C, 2, GQA_HEAD_DIM)),
    }


def reference(x, norm_g, ffn1_wg, ffn1_wu, ffn1_wd, ffn2_wg, ffn2_wu, ffn2_wd,
              diff_w_in, diff_w_out, diff_lambda, diff_subln,
              na_w_in, na_w_out, na_rpb,
              gqa_w_in, gqa_w_out, gqa_qk_norm):
    h = x
    for i in range(DEPTH):
        g = norm_g[i]
        h = h + FFN_RES * _rmsnorm(_swiglu(_rmsnorm(h, g[0]), ffn1_wg[i], ffn1_wu[i], ffn1_wd[i]), g[1])
        u = _rmsnorm(h, g[2])
        kind, j = i % N_MIXERS, i // N_MIXERS
        if kind == 0:
            m = _diff_attention(u, diff_w_in[j], diff_w_out[j], diff_lambda[j], diff_subln[j], i)
        elif kind == 1:
            m = _neighborhood_attention(u, na_w_in[j], na_w_out[j], na_rpb[j])
        else:
            m = _gqa_axial(u, gqa_w_in[j], gqa_w_out[j], gqa_qk_norm[j])
        h = h + _rmsnorm(m, g[3])
        h = h + FFN_RES * _rmsnorm(_swiglu(_rmsnorm(h, g[4]), ffn2_wg[i], ffn2_wu[i], ffn2_wd[i]), g[5])
    return h
```

```python
import functools
import math

import jax
import jax.numpy as jnp
from jax import lax
from jax.experimental import pallas as pl
from jax.experimental.pallas import tpu as pltpu

D_MODEL = 1024
SEQ = 8192
DEPTH = 4
N_MIXERS = 3
GRID_W = 64
EPS = 1e-6
D_FF = 2816
FFN_RES = 0.5
HEAD_DIM = 64
ROPE_THETA = 500000.0
ROPE_DIMS = 16
NA_WIN_H = 8
NA_WIN_W = 16
AXIAL_THETA = 10000.0

LOG2E = 1.4426950408889634
QSCALE = (HEAD_DIM ** -0.5) * LOG2E
NEG = -1e30

VMEM_LIMIT_BYTES = 56 * 1024 * 1024

BF16 = jnp.bfloat16
F32 = jnp.float32

_NT = (((1,), (1,)), ((), ()))


def _cparams(n_axes):
    return pltpu.CompilerParams(
        dimension_semantics=("parallel",) * n_axes,
        vmem_limit_bytes=VMEM_LIMIT_BYTES,
    )


def _resident(shape, index_map):
    return pl.BlockSpec(shape, index_map, pipeline_mode=pl.Buffered(1))


def _rms_rows(x, g):
    ms = jnp.mean(x * x, axis=-1, keepdims=True)
    return x * lax.rsqrt(ms + EPS) * g


def _ffn_kernel(x_ref, g_ref, wg_ref, wu_ref, wd_ref, o_ref):
    x = x_ref[...]
    g = g_ref[...]
    xn = _rms_rows(x, g[0:1]).astype(BF16)
    gate = jnp.dot(xn, wg_ref[...], preferred_element_type=F32)
    up = jnp.dot(xn, wu_ref[...], preferred_element_type=F32)
    act = (gate * jax.nn.sigmoid(gate) * up).astype(BF16)
    y = jnp.dot(act, wd_ref[...], preferred_element_type=F32)
    o_ref[...] = x + FFN_RES * _rms_rows(y, g[1:2])


def _ffn(h2d, g2, wg, wu, wd, tm=512):
    n = h2d.shape[0]
    return pl.pallas_call(
        _ffn_kernel,
        grid=(n // tm,),
        in_specs=[
            pl.BlockSpec((tm, D_MODEL), lambda i: (i, 0)),
            _resident((2, D_MODEL), lambda i: (0, 0)),
            _resident((D_MODEL, D_FF), lambda i: (0, 0)),
            _resident((D_MODEL, D_FF), lambda i: (0, 0)),
            _resident((D_FF, D_MODEL), lambda i: (0, 0)),
        ],
        out_specs=pl.BlockSpec((tm, D_MODEL), lambda i: (i, 0)),
        out_shape=jax.ShapeDtypeStruct(h2d.shape, F32),
        compiler_params=_cparams(1),
        name="ffn",
    )(h2d, g2, wg, wu, wd)


def _swap_halves(x, half):
    blocks = []
    r = 0
    n = x.shape[0]
    while r + 2 * half <= n:
        blocks += [x[r + half:r + 2 * half], x[r:r + half]]
        r += 2 * half
    return jnp.concatenate(blocks, axis=0)


def _inproj_t_kernel(h_ref, g_ref, wqT_ref, wkT_ref, wvT_ref, ct_ref, st_ref, gq_ref, gk_ref,
                     qT_ref, k_ref, vT_ref, *, variant):
    u = _rms_rows(h_ref[0], g_ref[...]).astype(BF16)
    ct = ct_ref[...]
    st = st_ref[...]

    def head(x, gain, scale):
        if variant == "gqa":
            ms = jnp.mean(x * x, axis=0, keepdims=True)
            x = x * lax.rsqrt(ms + EPS) * gain
            xs = _swap_halves(x, 16)
        else:
            xs = jnp.concatenate([x[8:16], x[0:8], x[16:64]], axis=0)
        x = x * ct + xs * st
        if scale is not None:
            x = x * scale
        return x

    gq = gq_ref[...]
    gk = gk_ref[...]

    qT = lax.dot_general(wqT_ref[...], u, _NT, preferred_element_type=F32)
    for hh in range(qT.shape[0] // HEAD_DIM):
        sl = slice(hh * HEAD_DIM, (hh + 1) * HEAD_DIM)
        qT_ref[0, sl, :] = head(qT[sl], gq, QSCALE).astype(BF16)

    kT = lax.dot_general(wkT_ref[...], u, _NT, preferred_element_type=F32)
    for pp in range(kT.shape[0] // 128):
        pair = jnp.concatenate(
            [head(kT[pp * 128 + e * 64: pp * 128 + (e + 1) * 64], gk, None) for e in range(2)], axis=0)
        k_ref[0, :, pp * 128:(pp + 1) * 128] = pair.T.astype(BF16)

    vT = lax.dot_general(wvT_ref[...], u, _NT, preferred_element_type=F32)
    vT_ref[0] = vT.astype(BF16)


def _inproj_t(h3d, g, wqT, wkT, wvT, ct, st, gq_b, gk_b, variant, tm=512):
    b, s, _ = h3d.shape
    nq, nk, nv = wqT.shape[0], wkT.shape[0], wvT.shape[0]
    kern = functools.partial(_inproj_t_kernel, variant=variant)
    return pl.pallas_call(
        kern,
        grid=(b, s // tm),
        in_specs=[
            pl.BlockSpec((1, tm, D_MODEL), lambda bi, ti: (bi, ti, 0)),
            _resident((1, D_MODEL), lambda bi, ti: (0, 0)),
            _resident((nq, D_MODEL), lambda bi, ti: (0, 0)),
            _resident((nk, D_MODEL), lambda bi, ti: (0, 0)),
            _resident((nv, D_MODEL), lambda bi, ti: (0, 0)),
            pl.BlockSpec((HEAD_DIM, tm), lambda bi, ti: (0, ti)),
            pl.BlockSpec((HEAD_DIM, tm), lambda bi, ti: (0, ti)),
            _resident((HEAD_DIM, tm), lambda bi, ti: (0, 0)),
            _resident((HEAD_DIM, tm), lambda bi, ti: (0, 0)),
        ],
        out_specs=[
            pl.BlockSpec((1, nq, tm), lambda bi, ti: (bi, 0, ti)),
            pl.BlockSpec((1, tm, nk), lambda bi, ti: (bi, ti, 0)),
            pl.BlockSpec((1, nv, tm), lambda bi, ti: (bi, 0, ti)),
        ],
        out_shape=[
            jax.ShapeDtypeStruct((b, nq, s), BF16),
            jax.ShapeDtypeStruct((b, s, nk), BF16),
            jax.ShapeDtypeStruct((b, nv, s), BF16),
        ],
        compiler_params=_cparams(2),
        name="inproj_" + variant,
    )(h3d, g, wqT, wkT, wvT, ct, st, gq_b, gk_b)


def _inproj_n_kernel(h_ref, g_ref, w_ref, q_ref, k_ref, v_ref):
    u = _rms_rows(h_ref[0], g_ref[...]).astype(BF16)
    y = jnp.dot(u, w_ref[...], preferred_element_type=F32)
    q_ref[0] = (y[:, 0:D_MODEL] * QSCALE).astype(BF16)
    k_ref[0] = y[:, D_MODEL:2 * D_MODEL].astype(BF16)
    v_ref[0] = y[:, 2 * D_MODEL:3 * D_MODEL].astype(BF16)


def _inproj_n(h3d, g, w, tm=512):
    b, s, _ = h3d.shape
    blk = pl.BlockSpec((1, tm, D_MODEL), lambda bi, ti: (bi, ti, 0))
    return pl.pallas_call(
        _inproj_n_kernel,
        grid=(b, s // tm),
        in_specs=[
            blk,
            _resident((1, D_MODEL), lambda bi, ti: (0, 0)),
            _resident((D_MODEL, 3 * D_MODEL), lambda bi, ti: (0, 0)),
        ],
        out_specs=[blk, blk, blk],
        out_shape=[jax.ShapeDtypeStruct((b, s, D_MODEL), BF16)] * 3,
        compiler_params=_cparams(2),
        name="inproj_na",
    )(h3d, g, w)


def _flash_kernel(qT_ref, k_ref, vT_ref, *rest, variant, qb, kb, nk, lam_init):
    if variant == "diff":
        lam_ref, gsub_ref, o_ref, m_sc, l_sc, acc_sc = rest
    else:
        o_ref, m_sc, l_sc, acc_sc = rest

    qT = qT_ref[0]
    zero = jnp.zeros((HEAD_DIM, qb), BF16)
    if variant == "diff":
        qz = jnp.concatenate(
            [jnp.concatenate([qT[0:64], zero], axis=0),
             jnp.concatenate([zero, qT[64:128]], axis=0)], axis=1)
    else:
        parity = pl.program_id(1) % 2
        row_half = (lax.broadcasted_iota(jnp.int32, (128, qb), 0) >= HEAD_DIM).astype(jnp.int32)
        keep = row_half == parity
        parts = []
        for g in range(4):
            qg = qT[g * 64:(g + 1) * 64]
            parts.append(jnp.where(keep, jnp.concatenate([qg, qg], axis=0), jnp.zeros((), BF16)))
        qz = jnp.concatenate(parts, axis=1)

    m_sc[...] = jnp.full(m_sc.shape, NEG, F32)
    l_sc[...] = jnp.zeros(l_sc.shape, F32)
    acc_sc[...] = jnp.zeros(acc_sc.shape, F32)

    def body(i, carry):
        ks = pl.multiple_of(i * kb, kb)
        kblk = k_ref[0, pl.ds(ks, kb), :]
        s = jnp.dot(kblk, qz, preferred_element_type=F32)
        m_prev = m_sc[...]
        m_new = jnp.maximum(m_prev, jnp.max(s, axis=0, keepdims=True))
        alpha = jnp.exp2(m_prev - m_new)
        p = jnp.exp2(s - m_new)
        l_sc[...] = alpha * l_sc[...] + jnp.sum(p, axis=0, keepdims=True)
        vblk = vT_ref[0, :, pl.ds(ks, kb)]
        acc_sc[...] = alpha * acc_sc[...] + jnp.dot(vblk, p.astype(BF16), preferred_element_type=F32)
        m_sc[...] = m_new
        return carry

    lax.fori_loop(0, nk, body, 0)

    o = acc_sc[...] / l_sc[...]
    if variant == "diff":
        lv = lam_ref[...]
        lam = (jnp.exp(jnp.sum(lv[0:1] * lv[1:2], axis=1, keepdims=True))
               - jnp.exp(jnp.sum(lv[2:3] * lv[3:4], axis=1, keepdims=True)) + lam_init)
        od = o[:, 0:qb] - lam * o[:, qb:2 * qb]
        ms = jnp.mean(od * od, axis=0, keepdims=True)
        od = od * lax.rsqrt(ms + EPS) * gsub_ref[...] * (1.0 - lam_init)
        o_ref[0] = od.T.astype(BF16)
    else:
        stacked = jnp.concatenate([o[:, g * qb:(g + 1) * qb] for g in range(4)], axis=0)
        o_ref[0] = stacked.T.astype(BF16)


def _flash(qT, k, vT, variant, lam_vecs=None, gsub_b=None, lam_init=0.0, kb=256):
    b, _, s = qT.shape
    if variant == "diff":
        qb, n_groups, r, dv, width = 256, 8, 128, 128, 128
        lanes = 2 * qb
        k_map = lambda bi, j, qi: (bi, 0, j)
    else:
        qb, n_groups, r, dv, width = 128, 4, 256, 64, 256
        lanes = 4 * qb
        k_map = lambda bi, j, qi: (bi, 0, j // 2)
    nk = s // kb
    kern = functools.partial(_flash_kernel, variant=variant, qb=qb, kb=kb, nk=nk, lam_init=lam_init)
    in_specs = [
        pl.BlockSpec((1, r, qb), lambda bi, j, qi: (bi, j, qi)),
        pl.BlockSpec((1, s, 128), k_map),
        pl.BlockSpec((1, dv, s), lambda bi, j, qi: (bi, j, 0)),
    ]
    args = [qT, k, vT]
    if variant == "diff":
        in_specs += [
            _resident((4, HEAD_DIM), lambda bi, j, qi: (0, 0)),
            _resident((128, qb), lambda bi, j, qi: (0, 0)),
        ]
        args += [lam_vecs, gsub_b]
    return pl.pallas_call(
        kern,
        grid=(b, n_groups, s // qb),
        in_specs=in_specs,
        out_specs=pl.BlockSpec((1, qb, width), lambda bi, j, qi: (bi, qi, j)),
        out_shape=jax.ShapeDtypeStruct((b, s, D_MODEL), BF16),
        scratch_shapes=[
            pltpu.VMEM((1, lanes), F32),
            pltpu.VMEM((1, lanes), F32),
            pltpu.VMEM((dv, lanes), F32),
        ],
        compiler_params=_cparams(3),
        name="flash_" + variant,
    )(*args)


def _na_table_kernel(rpb_ref, pt_ref):
    cq = lax.broadcasted_iota(jnp.int32, (GRID_W, 128), 0)
    ck = lax.broadcasted_iota(jnp.int32, (GRID_W, 128), 1) & (GRID_W - 1)
    c0 = jnp.clip(cq - NA_WIN_W // 2, 0, GRID_W - NA_WIN_W)
    valid = (ck >= c0) & (ck < c0 + NA_WIN_W)
    for dr in range(2 * NA_WIN_H - 2):
        a = rpb_ref[0, dr:dr + 1, :]
        bb = rpb_ref[0, dr + 1:dr + 2, :]
        x = a + pltpu.roll(bb, GRID_W, axis=1)
        xb = jnp.broadcast_to(x, (GRID_W, 128))
        t = pltpu.roll(xb, 128 - (NA_WIN_W - 1), axis=1, stride=1, stride_axis=0)
        pt_ref[0, dr] = jnp.where(valid, t * LOG2E, NEG)


def _na_table(rpb_pad):
    nh = rpb_pad.shape[0]
    ndr = 2 * NA_WIN_H - 2
    return pl.pallas_call(
        _na_table_kernel,
        grid=(nh,),
        in_specs=[pl.BlockSpec((1, 16, 128), lambda h: (h, 0, 0))],
        out_specs=pl.BlockSpec((1, ndr, GRID_W, 128), lambda h: (h, 0, 0, 0)),
        out_shape=jax.ShapeDtypeStruct((nh, ndr, GRID_W, 128), F32),
        compiler_params=_cparams(1),
        name="na_table",
    )(rpb_pad)


_NA_ROWS = 4
_NA_BAND = _NA_ROWS + NA_WIN_H


def _na_kernel(q_ref, k0_ref, k1_ref, k2_ref, v0_ref, v1_ref, v2_ref, pt_ref, o_ref, kband, vband,
               *, n_steps):
    i = pl.program_id(1)
    blk = _NA_ROWS * GRID_W
    for n, (kr, vr) in enumerate(((k0_ref, v0_ref), (k1_ref, v1_ref), (k2_ref, v2_ref))):
        kband[n * blk:(n + 1) * blk, :] = kr[0]
        vband[n * blk:(n + 1) * blk, :] = vr[0]
    left = lax.broadcasted_iota(jnp.int32, (GRID_W, 128), 1) < HEAD_DIM
    nkeys = NA_WIN_H * GRID_W

    def row_body(t, carry):
        off = jnp.where(i == 0, 0, jnp.where(i == n_steps - 1, _NA_ROWS, t))
        sidx = jnp.where(i == 0, NA_WIN_H - 1 - t, jnp.where(i == n_steps - 1, 3 - t, 3))
        start = pl.multiple_of(off * GRID_W, GRID_W)
        qs = pl.multiple_of(t * GRID_W, GRID_W)
        for j in range(D_MODEL // 128):
            cs = slice(j * 128, (j + 1) * 128)
            qp = q_ref[0, pl.ds(qs, GRID_W), cs]
            kp = kband[pl.ds(start, nkeys), cs]
            vp = vband[pl.ds(start, nkeys), cs]
            outs = []
            for e in range(2):
                h = 2 * j + e
                qz = jnp.where(left if e == 0 else jnp.logical_not(left), qp, jnp.zeros((), BF16))
                s = lax.dot_general(qz, kp, _NT, preferred_element_type=F32)
                bias = jnp.concatenate([pt_ref[h, sidx + 2 * m] for m in range(NA_WIN_H // 2)], axis=1)
                s = s + bias
                mx = jnp.max(s, axis=1, keepdims=True)
                p = jnp.exp2(s - mx)
                l = jnp.sum(p, axis=1, keepdims=True)
                o = jnp.dot(p.astype(BF16), vp, preferred_element_type=F32)
                outs.append(o / l)
            o_ref[0, pl.ds(qs, GRID_W), cs] = jnp.where(left, outs[0], outs[1]).astype(BF16)
        return carry

    lax.fori_loop(0, _NA_ROWS, row_body, 0)


def _na_attention(q, k, v, pt):
    b, s, _ = q.shape
    blk = _NA_ROWS * GRID_W
    n_steps = s // blk
    last = n_steps - 3

    def band_spec(n):
        return pl.BlockSpec((1, blk, D_MODEL),
                            lambda bi, i: (bi, jnp.clip(i - 1, 0, last) + n, 0))

    kern = functools.partial(_na_kernel, n_steps=n_steps)
    return pl.pallas_call(
        kern,
        grid=(b, n_steps),
        in_specs=[
            pl.BlockSpec((1, blk, D_MODEL), lambda bi, i: (bi, i, 0)),
            band_spec(0), band_spec(1), band_spec(2),
            band_spec(0), band_spec(1), band_spec(2),
            _resident(pt.shape, lambda bi, i: (0, 0, 0, 0)),
        ],
        out_specs=pl.BlockSpec((1, blk, D_MODEL), lambda bi, i: (bi, i, 0)),
        out_shape=jax.ShapeDtypeStruct((b, s, D_MODEL), BF16),
        scratch_shapes=[
            pltpu.VMEM((_NA_BAND * GRID_W, D_MODEL), BF16),
            pltpu.VMEM((_NA_BAND * GRID_W, D_MODEL), BF16),
        ],
        compiler_params=_cparams(2),
        name="na_attention",
    )(q, k, k, k, v, v, v, pt)


def _outproj_kernel(o_ref, h_ref, g_ref, w_ref, out_ref):
    m = jnp.dot(o_ref[...], w_ref[...], preferred_element_type=F32)
    out_ref[...] = h_ref[...] + _rms_rows(m, g_ref[...])


def _outproj(o2d, h2d, g, w, tm=512):
    n = h2d.shape[0]
    blk = pl.BlockSpec((tm, D_MODEL), lambda i: (i, 0))
    return pl.pallas_call(
        _outproj_kernel,
        grid=(n // tm,),
        in_specs=[
            blk, blk,
            _resident((1, D_MODEL), lambda i: (0, 0)),
            _resident((D_MODEL, D_MODEL), lambda i: (0, 0)),
        ],
        out_specs=blk,
        out_shape=jax.ShapeDtypeStruct(h2d.shape, F32),
        compiler_params=_cparams(1),
        name="outproj",
    )(o2d, h2d, g, w)


def _angles(pos, dims, theta):
    inv = theta ** (-jnp.arange(0, dims, 2, dtype=F32) / dims)
    return pos[:, None] * inv[None, :]


def _diff_tables(s):
    ang = _angles(jnp.arange(s, dtype=F32), ROPE_DIMS, ROPE_THETA)
    c, sn = jnp.cos(ang).T, jnp.sin(ang).T
    ones = jnp.ones((HEAD_DIM - ROPE_DIMS, s), F32)
    ct = jnp.concatenate([c, c, ones], axis=0)
    st = jnp.concatenate([-sn, sn, 0.0 * ones], axis=0)
    return ct, st


def _axial_tables(s):
    t = jnp.arange(s)
    half = HEAD_DIM // 2
    ar = _angles((t // GRID_W).astype(F32), half, AXIAL_THETA)
    ac = _angles((t % GRID_W).astype(F32), half, AXIAL_THETA)
    cr, sr, cc, sc = jnp.cos(ar).T, jnp.sin(ar).T, jnp.cos(ac).T, jnp.sin(ac).T
    ct = jnp.concatenate([cr, cr, cc, cc], axis=0)
    st = jnp.concatenate([-sr, sr, -sc, sc], axis=0)
    return ct, st


def _col_gain(g, tm):
    return jnp.broadcast_to(g.astype(F32)[:, None], (g.shape[0], tm))


def kernel(x, norm_g, ffn1_wg, ffn1_wu, ffn1_wd, ffn2_wg, ffn2_wu, ffn2_wd,
           diff_w_in, diff_w_out, diff_lambda, diff_subln,
           na_w_in, na_w_out, na_rpb,
           gqa_w_in, gqa_w_out, gqa_qk_norm):
    b, s, d = x.shape
    n = b * s
    tm = 512
    h = x.reshape(n, d)
    ones_gain = jnp.ones((HEAD_DIM, tm), F32)

    for i in range(DEPTH):
        g = norm_g[i]
        h = _ffn(h, g[0:2], ffn1_wg[i].astype(BF16), ffn1_wu[i].astype(BF16), ffn1_wd[i].astype(BF16))
        kind, j = i % N_MIXERS, i // N_MIXERS
        h3 = h.reshape(b, s, d)
        if kind == 0:
            w = diff_w_in[j]
            wqT = w[:, 0:d].T.astype(BF16)
            wkT = w[:, d:2 * d].T.astype(BF16)
            wvT = w[:, 2 * d:3 * d].T.astype(BF16)
            ct, st = _diff_tables(s)
            qT, k, vT = _inproj_t(h3, g[2:3], wqT, wkT, wvT, ct, st, ones_gain, ones_gain, "diff", tm)
            lam_init = 0.8 - 0.6 * math.exp(-0.3 * i)
            gsub_b = _col_gain(diff_subln[j], 256)
            o = _flash(qT, k, vT, "diff", diff_lambda[j].astype(F32), gsub_b, lam_init)
            w_out = diff_w_out[j]
        elif kind == 1:
            q, k, v = _inproj_n(h3, g[2:3], na_w_in[j].astype(BF16), tm)
            rpb = na_rpb[j].astype(F32)
            rpb_pad = jnp.zeros((rpb.shape[0], 16, 128), F32).at[:, :rpb.shape[1], :rpb.shape[2]].set(rpb)
            pt = _na_table(rpb_pad)
            o = _na_attention(q, k, v, pt)
            w_out = na_w_out[j]
        else:
            w = gqa_w_in[j]
            nq = d
            nkv = (w.shape[1] - nq) // 2
            wqT = w[:, 0:nq].T.astype(BF16)
            wkT = w[:, nq:nq + nkv].T.astype(BF16)
            wvT = w[:, nq + nkv:].T.astype(BF16)
            ct, st = _axial_tables(s)
            gq_b = _col_gain(gqa_qk_norm[j, 0], tm)
            gk_b = _col_gain(gqa_qk_norm[j, 1], tm)
            qT, k, vT = _inproj_t(h3, g[2:3], wqT, wkT, wvT, ct, st, gq_b, gk_b, "gqa", tm)
            o = _flash(qT, k, vT, "gqa")
            w_out = gqa_w_out[j]
        h = _outproj(o.reshape(n, d), h, g[3:4], w_out.astype(BF16), tm)
        h = _ffn(h, g[4:6], ffn2_wg[i].astype(BF16), ffn2_wu[i].astype(BF16), ffn2_wd[i].astype(BF16))
    return h.reshape(b, s, d)
```

```python
import functools
import math

import jax
import jax.numpy as jnp
from jax import lax
from jax.experimental import pallas as pl
from jax.experimental.pallas import tpu as pltpu

D_MODEL = 1024
SEQ = 8192
DEPTH = 4
N_MIXERS = 3
GRID_W = 64
EPS = 1e-6
D_FF = 2816
FFN_RES = 0.5
HEAD_DIM = 64
ROPE_THETA = 500000.0
ROPE_DIMS = 16
NA_WIN_H = 8
NA_WIN_W = 16
AXIAL_THETA = 10000.0

LOG2E = 1.4426950408889634
QSCALE = (HEAD_DIM ** -0.5) * LOG2E
NEG = -1e30

VMEM_LIMIT_BYTES = 56 * 1024 * 1024

BF16 = jnp.bfloat16
F32 = jnp.float32

_NT = (((1,), (1,)), ((), ()))


def _cparams(n_axes):
    return pltpu.CompilerParams(
        dimension_semantics=("parallel",) * n_axes,
        vmem_limit_bytes=VMEM_LIMIT_BYTES,
    )


def _resident(shape, index_map):
    return pl.BlockSpec(shape, index_map, pipeline_mode=pl.Buffered(1))


def _rms_rows(x, g):
    ms = jnp.mean(x * x, axis=-1, keepdims=True)
    return x * lax.rsqrt(ms + EPS) * g


def _ffn_kernel(x_ref, g_ref, wg_ref, wu_ref, wd_ref, o_ref):
    x = x_ref[...]
    g = g_ref[...]
    xn = _rms_rows(x, g[0:1]).astype(BF16)
    gate = jnp.dot(xn, wg_ref[...], preferred_element_type=F32)
    up = jnp.dot(xn, wu_ref[...], preferred_element_type=F32)
    act = (gate * jax.nn.sigmoid(gate) * up).astype(BF16)
    y = jnp.dot(act, wd_ref[...], preferred_element_type=F32)
    o_ref[...] = x + FFN_RES * _rms_rows(y, g[1:2])


def _ffn(h2d, g2, wg, wu, wd, tm=512):
    n = h2d.shape[0]
    return pl.pallas_call(
        _ffn_kernel,
        grid=(n // tm,),
        in_specs=[
            pl.BlockSpec((tm, D_MODEL), lambda i: (i, 0)),
            _resident((2, D_MODEL), lambda i: (0, 0)),
            _resident((D_MODEL, D_FF), lambda i: (0, 0)),
            _resident((D_MODEL, D_FF), lambda i: (0, 0)),
            _resident((D_FF, D_MODEL), lambda i: (0, 0)),
        ],
        out_specs=pl.BlockSpec((tm, D_MODEL), lambda i: (i, 0)),
        out_shape=jax.ShapeDtypeStruct(h2d.shape, F32),
        compiler_params=_cparams(1),
        name="ffn",
    )(h2d, g2, wg, wu, wd)


def _swap_halves(x, half):
    blocks = []
    r = 0
    n = x.shape[0]
    while r + 2 * half <= n:
        blocks += [x[r + half:r + 2 * half], x[r:r + half]]
        r += 2 * half
    return jnp.concatenate(blocks, axis=0)


def _inproj_t_kernel(h_ref, g_ref, wqT_ref, wkT_ref, wvT_ref, ct_ref, st_ref, gq_ref, gk_ref,
                     qT_ref, k_ref, vT_ref, *, variant):
    u = _rms_rows(h_ref[0], g_ref[...]).astype(BF16)
    ct = ct_ref[...]
    st = st_ref[...]

    def head(x, gain, scale):
        if variant == "gqa":
            ms = jnp.mean(x * x, axis=0, keepdims=True)
            x = x * lax.rsqrt(ms + EPS) * gain
            xs = _swap_halves(x, 16)
        else:
            xs = jnp.concatenate([x[8:16], x[0:8], x[16:64]], axis=0)
        x = x * ct + xs * st
        if scale is not None:
            x = x * scale
        return x

    gq = gq_ref[...]
    gk = gk_ref[...]

    qT = lax.dot_general(wqT_ref[...], u, _NT, preferred_element_type=F32)
    for hh in range(qT.shape[0] // HEAD_DIM):
        sl = slice(hh * HEAD_DIM, (hh + 1) * HEAD_DIM)
        qT_ref[0, sl, :] = head(qT[sl], gq, QSCALE).astype(BF16)

    kT = lax.dot_general(wkT_ref[...], u, _NT, preferred_element_type=F32)
    for pp in range(kT.shape[0] // 128):
        pair = jnp.concatenate(
            [head(kT[pp * 128 + e * 64: pp * 128 + (e + 1) * 64], gk, None) for e in range(2)], axis=0)
        k_ref[0, :, pp * 128:(pp + 1) * 128] = pair.T.astype(BF16)

    vT = lax.dot_general(wvT_ref[...], u, _NT, preferred_element_type=F32)
    vT_ref[0] = vT.astype(BF16)


def _inproj_t(h3d, g, wqT, wkT, wvT, ct, st, gq_b, gk_b, variant, tm=512):
    b, s, _ = h3d.shape
    nq, nk, nv = wqT.shape[0], wkT.shape[0], wvT.shape[0]
    kern = functools.partial(_inproj_t_kernel, variant=variant)
    return pl.pallas_call(
        kern,
        grid=(b, s // tm),
        in_specs=[
            pl.BlockSpec((1, tm, D_MODEL), lambda bi, ti: (bi, ti, 0)),
            _resident((1, D_MODEL), lambda bi, ti: (0, 0)),
            _resident((nq, D_MODEL), lambda bi, ti: (0, 0)),
            _resident((nk, D_MODEL), lambda bi, ti: (0, 0)),
            _resident((nv, D_MODEL), lambda bi, ti: (0, 0)),
            pl.BlockSpec((HEAD_DIM, tm), lambda bi, ti: (0, ti)),
            pl.BlockSpec((HEAD_DIM, tm), lambda bi, ti: (0, ti)),
            _resident((HEAD_DIM, tm), lambda bi, ti: (0, 0)),
            _resident((HEAD_DIM, tm), lambda bi, ti: (0, 0)),
        ],
        out_specs=[
            pl.BlockSpec((1, nq, tm), lambda bi, ti: (bi, 0, ti)),
            pl.BlockSpec((1, tm, nk), lambda bi, ti: (bi, ti, 0)),
            pl.BlockSpec((1, nv, tm), lambda bi, ti: (bi, 0, ti)),
        ],
        out_shape=[
            jax.ShapeDtypeStruct((b, nq, s), BF16),
            jax.ShapeDtypeStruct((b, s, nk), BF16),
            jax.ShapeDtypeStruct((b, nv, s), BF16),
        ],
        compiler_params=_cparams(2),
        name="inproj_" + variant,
    )(h3d, g, wqT, wkT, wvT, ct, st, gq_b, gk_b)


def _inproj_n_kernel(h_ref, g_ref, w_ref, q_ref, k_ref, v_ref):
    u = _rms_rows(h_ref[0], g_ref[...]).astype(BF16)
    y = jnp.dot(u, w_ref[...], preferred_element_type=F32)
    q_ref[0] = (y[:, 0:D_MODEL] * QSCALE).astype(BF16)
    k_ref[0] = y[:, D_MODEL:2 * D_MODEL].astype(BF16)
    v_ref[0] = y[:, 2 * D_MODEL:3 * D_MODEL].astype(BF16)


def _inproj_n(h3d, g, w, tm=512):
    b, s, _ = h3d.shape
    blk = pl.BlockSpec((1, tm, D_MODEL), lambda bi, ti: (bi, ti, 0))
    return pl.pallas_call(
        _inproj_n_kernel,
        grid=(b, s // tm),
        in_specs=[
            blk,
            _resident((1, D_MODEL), lambda bi, ti: (0, 0)),
            _resident((D_MODEL, 3 * D_MODEL), lambda bi, ti: (0, 0)),
        ],
        out_specs=[blk, blk, blk],
        out_shape=[jax.ShapeDtypeStruct((b, s, D_MODEL), BF16)] * 3,
        compiler_params=_cparams(2),
        name="inproj_na",
    )(h3d, g, w)


_SUM_ROWS = 16
_RING = 4
_LOOKAHEAD = 3
_FLASH_LANES = 1024


def _flash_kernel(qT_ref, k_ref, vT_ref, *rest, variant, qb, kb, nk, unroll, lam_init):
    if variant == "diff":
        lam_ref, gsub_ref, o_ref, s_ring, p_ring, acc_sc = rest
    else:
        o_ref, s_ring, p_ring, acc_sc = rest

    qT = qT_ref[0]
    zero = jnp.zeros((HEAD_DIM, qb), BF16)
    if variant == "diff":
        qz = jnp.concatenate(
            [jnp.concatenate([qT[0:64], zero], axis=0),
             jnp.concatenate([zero, qT[64:128]], axis=0)], axis=1)
    else:
        parity = pl.program_id(1) % 2
        row_half = (lax.broadcasted_iota(jnp.int32, (128, qb), 0) >= HEAD_DIM).astype(jnp.int32)
        keep = row_half == parity
        parts = []
        for g in range(4):
            qg = qT[g * 64:(g + 1) * 64]
            parts.append(jnp.where(keep, jnp.concatenate([qg, qg], axis=0), jnp.zeros((), BF16)))
        qz = jnp.concatenate(parts, axis=1)

    lanes = qz.shape[1]

    def scores(i):
        ks = pl.multiple_of(i * kb, kb)
        return jnp.dot(k_ref[0, pl.ds(ks, kb), :], qz, preferred_element_type=F32)

    dv = vT_ref.shape[1]
    ones_rows = jnp.ones((_SUM_ROWS, kb), BF16)

    def weighted_values(i, p):
        ks = pl.multiple_of(i * kb, kb)
        v_ext = jnp.concatenate([vT_ref[0, :, pl.ds(ks, kb)], ones_rows], axis=0)
        return jnp.dot(v_ext, p, preferred_element_type=F32)

    def stage(i, slot, stats, with_scores=True):
        alpha_prev, m_prev, maxes = stats
        if with_scores:
            s_new = scores(i + _LOOKAHEAD)
            s_ring[(slot + _LOOKAHEAD) % _RING] = s_new
            maxes = maxes[1:] + (jnp.max(s_new, axis=0, keepdims=True),)
        else:
            maxes = maxes[1:] + (maxes[0],)
        acc_sc[...] = alpha_prev * acc_sc[...] + weighted_values(jnp.maximum(i - 1, 0), p_ring[(slot + 1) % 2])
        m_new = jnp.maximum(m_prev, stats[2][0])
        p_ring[slot % 2] = jnp.exp2(s_ring[slot % _RING] - m_new).astype(BF16)
        return jnp.exp2(m_prev - m_new), m_new, maxes

    def body(ii, stats):
        for u in range(unroll):
            stats = stage(unroll * ii + u, u, stats)
        return stats

    first = [scores(t) for t in range(_LOOKAHEAD)]
    for t in range(_LOOKAHEAD):
        s_ring[t] = first[t]
    p_ring[1] = jnp.zeros(p_ring.shape[1:], BF16)
    acc_sc[...] = jnp.zeros(acc_sc.shape, F32)
    stats = (jnp.ones((1, lanes), F32), jnp.full((1, lanes), NEG, F32),
             tuple(jnp.max(f, axis=0, keepdims=True) for f in first))
    n_body = nk // unroll
    stats = lax.fori_loop(0, n_body - 1, body, stats)
    for u in range(unroll):
        i_last = unroll * (n_body - 1) + u
        stats = stage(i_last, u, stats, with_scores=i_last + _LOOKAHEAD < nk)
    acc = stats[0] * acc_sc[...] + weighted_values(nk - 1, p_ring[(nk - 1) % 2])

    o = acc[0:dv] / acc[dv:dv + 1]
    if variant == "diff":
        lv = lam_ref[...]
        lam = (jnp.exp(jnp.sum(lv[0:1] * lv[1:2], axis=1, keepdims=True))
               - jnp.exp(jnp.sum(lv[2:3] * lv[3:4], axis=1, keepdims=True)) + lam_init)
        od = o[:, 0:qb] - lam * o[:, qb:2 * qb]
        ms = jnp.mean(od * od, axis=0, keepdims=True)
        od = od * lax.rsqrt(ms + EPS) * gsub_ref[...] * (1.0 - lam_init)
        o_ref[0] = od.T.astype(BF16)
    else:
        stacked = jnp.concatenate([o[:, g * qb:(g + 1) * qb] for g in range(4)], axis=0)
        o_ref[0] = stacked.T.astype(BF16)


def _flash(qT, k, vT, variant, lam_vecs=None, gsub_b=None, lam_init=0.0, kb=256, unroll=8):
    b, _, s = qT.shape
    lanes = _FLASH_LANES
    if variant == "diff":
        qb, n_groups, r, dv, width = lanes // 2, 8, 128, 128, 128
        k_map = lambda bi, j, qi: (bi, 0, j)
    else:
        qb, n_groups, r, dv, width = lanes // 4, 4, 256, 64, 256
        k_map = lambda bi, j, qi: (bi, 0, j // 2)
    nk = s // kb
    assert unroll % _RING == 0 and nk % unroll == 0
    kern = functools.partial(_flash_kernel, variant=variant, qb=qb, kb=kb, nk=nk, unroll=unroll,
                             lam_init=lam_init)
    in_specs = [
        pl.BlockSpec((1, r, qb), lambda bi, j, qi: (bi, j, qi)),
        pl.BlockSpec((1, s, 128), k_map),
        pl.BlockSpec((1, dv, s), lambda bi, j, qi: (bi, j, 0)),
    ]
    args = [qT, k, vT]
    if variant == "diff":
        in_specs += [
            _resident((4, HEAD_DIM), lambda bi, j, qi: (0, 0)),
            _resident((128, qb), lambda bi, j, qi: (0, 0)),
        ]
        args += [lam_vecs, gsub_b]
    return pl.pallas_call(
        kern,
        grid=(b, n_groups, s // qb),
        in_specs=in_specs,
        out_specs=pl.BlockSpec((1, qb, width), lambda bi, j, qi: (bi, qi, j)),
        out_shape=jax.ShapeDtypeStruct((b, s, D_MODEL), BF16),
        scratch_shapes=[
            pltpu.VMEM((_RING, kb, lanes), F32),
            pltpu.VMEM((2, kb, lanes), BF16),
            pltpu.VMEM((dv + _SUM_ROWS, lanes), F32),
        ],
        compiler_params=_cparams(3),
        name="flash_" + variant,
    )(*args)


def _na_table_kernel(rpb_ref, pt_ref):
    cq = lax.broadcasted_iota(jnp.int32, (GRID_W, 128), 0)
    ck = lax.broadcasted_iota(jnp.int32, (GRID_W, 128), 1) & (GRID_W - 1)
    c0 = jnp.clip(cq - NA_WIN_W // 2, 0, GRID_W - NA_WIN_W)
    valid = (ck >= c0) & (ck < c0 + NA_WIN_W)
    for dr in range(2 * NA_WIN_H - 2):
        a = rpb_ref[0, dr:dr + 1, :]
        bb = rpb_ref[0, dr + 1:dr + 2, :]
        x = a + pltpu.roll(bb, GRID_W, axis=1)
        xb = jnp.broadcast_to(x, (GRID_W, 128))
        t = pltpu.roll(xb, 128 - (NA_WIN_W - 1), axis=1, stride=1, stride_axis=0)
        pt_ref[0, dr] = jnp.where(valid, t * LOG2E, NEG)


def _na_table(rpb_pad):
    nh = rpb_pad.shape[0]
    ndr = 2 * NA_WIN_H - 2
    return pl.pallas_call(
        _na_table_kernel,
        grid=(nh,),
        in_specs=[pl.BlockSpec((1, 16, 128), lambda h: (h, 0, 0))],
        out_specs=pl.BlockSpec((1, ndr, GRID_W, 128), lambda h: (h, 0, 0, 0)),
        out_shape=jax.ShapeDtypeStruct((nh, ndr, GRID_W, 128), F32),
        compiler_params=_cparams(1),
        name="na_table",
    )(rpb_pad)


_NA_ROWS = 4
_NA_BAND = _NA_ROWS + NA_WIN_H


def _na_kernel(q_ref, k0_ref, k1_ref, k2_ref, v0_ref, v1_ref, v2_ref, pt_ref, o_ref, kband, vband,
               *, n_steps):
    i = pl.program_id(1)
    blk = _NA_ROWS * GRID_W
    for n, (kr, vr) in enumerate(((k0_ref, v0_ref), (k1_ref, v1_ref), (k2_ref, v2_ref))):
        kband[n * blk:(n + 1) * blk, :] = kr[0]
        vband[n * blk:(n + 1) * blk, :] = vr[0]
    left = lax.broadcasted_iota(jnp.int32, (GRID_W, 128), 1) < HEAD_DIM
    nkeys = NA_WIN_H * GRID_W

    def row_body(t, carry):
        off = jnp.where(i == 0, 0, jnp.where(i == n_steps - 1, _NA_ROWS, t))
        sidx = jnp.where(i == 0, NA_WIN_H - 1 - t, jnp.where(i == n_steps - 1, 3 - t, 3))
        start = pl.multiple_of(off * GRID_W, GRID_W)
        qs = pl.multiple_of(t * GRID_W, GRID_W)
        for j in range(D_MODEL // 128):
            cs = slice(j * 128, (j + 1) * 128)
            qp = q_ref[0, pl.ds(qs, GRID_W), cs]
            kp = kband[pl.ds(start, nkeys), cs]
            vp = vband[pl.ds(start, nkeys), cs]
            outs = []
            for e in range(2):
                h = 2 * j + e
                qz = jnp.where(left if e == 0 else jnp.logical_not(left), qp, jnp.zeros((), BF16))
                s = lax.dot_general(qz, kp, _NT, preferred_element_type=F32)
                bias = jnp.concatenate([pt_ref[h, sidx + 2 * m] for m in range(NA_WIN_H // 2)], axis=1)
                s = s + bias
                mx = jnp.max(s, axis=1, keepdims=True)
                p = jnp.exp2(s - mx)
                l = jnp.sum(p, axis=1, keepdims=True)
                o = jnp.dot(p.astype(BF16), vp, preferred_element_type=F32)
                outs.append(o / l)
            o_ref[0, pl.ds(qs, GRID_W), cs] = jnp.where(left, outs[0], outs[1]).astype(BF16)
        return carry

    lax.fori_loop(0, _NA_ROWS, row_body, 0)


def _na_attention(q, k, v, pt):
    b, s, _ = q.shape
    blk = _NA_ROWS * GRID_W
    n_steps = s // blk
    last = n_steps - 3

    def band_spec(n):
        return pl.BlockSpec((1, blk, D_MODEL),
                            lambda bi, i: (bi, jnp.clip(i - 1, 0, last) + n, 0))

    kern = functools.partial(_na_kernel, n_steps=n_steps)
    return pl.pallas_call(
        kern,
        grid=(b, n_steps),
        in_specs=[
            pl.BlockSpec((1, blk, D_MODEL), lambda bi, i: (bi, i, 0)),
            band_spec(0), band_spec(1), band_spec(2),
            band_spec(0), band_spec(1), band_spec(2),
            _resident(pt.shape, lambda bi, i: (0, 0, 0, 0)),
        ],
        out_specs=pl.BlockSpec((1, blk, D_MODEL), lambda bi, i: (bi, i, 0)),
        out_shape=jax.ShapeDtypeStruct((b, s, D_MODEL), BF16),
        scratch_shapes=[
            pltpu.VMEM((_NA_BAND * GRID_W, D_MODEL), BF16),
            pltpu.VMEM((_NA_BAND * GRID_W, D_MODEL), BF16),
        ],
        compiler_params=_cparams(2),
        name="na_attention",
    )(q, k, k, k, v, v, v, pt)


def _outproj_kernel(o_ref, h_ref, g_ref, w_ref, out_ref):
    m = jnp.dot(o_ref[...], w_ref[...], preferred_element_type=F32)
    out_ref[...] = h_ref[...] + _rms_rows(m, g_ref[...])


def _outproj(o2d, h2d, g, w, tm=512):
    n = h2d.shape[0]
    blk = pl.BlockSpec((tm, D_MODEL), lambda i: (i, 0))
    return pl.pallas_call(
        _outproj_kernel,
        grid=(n // tm,),
        in_specs=[
            blk, blk,
            _resident((1, D_MODEL), lambda i: (0, 0)),
            _resident((D_MODEL, D_MODEL), lambda i: (0, 0)),
        ],
        out_specs=blk,
        out_shape=jax.ShapeDtypeStruct(h2d.shape, F32),
        compiler_params=_cparams(1),
        name="outproj",
    )(o2d, h2d, g, w)


def _angles(pos, dims, theta):
    inv = theta ** (-jnp.arange(0, dims, 2, dtype=F32) / dims)
    return pos[:, None] * inv[None, :]


def _diff_tables(s):
    ang = _angles(jnp.arange(s, dtype=F32), ROPE_DIMS, ROPE_THETA)
    c, sn = jnp.cos(ang).T, jnp.sin(ang).T
    ones = jnp.ones((HEAD_DIM - ROPE_DIMS, s), F32)
    ct = jnp.concatenate([c, c, ones], axis=0)
    st = jnp.concatenate([-sn, sn, 0.0 * ones], axis=0)
    return ct, st


def _axial_tables(s):
    t = jnp.arange(s)
    half = HEAD_DIM // 2
    ar = _angles((t // GRID_W).astype(F32), half, AXIAL_THETA)
    ac = _angles((t % GRID_W).astype(F32), half, AXIAL_THETA)
    cr, sr, cc, sc = jnp.cos(ar).T, jnp.sin(ar).T, jnp.cos(ac).T, jnp.sin(ac).T
    ct = jnp.concatenate([cr, cr, cc, cc], axis=0)
    st = jnp.concatenate([-sr, sr, -sc, sc], axis=0)
    return ct, st


def _col_gain(g, tm):
    return jnp.broadcast_to(g.astype(F32)[:, None], (g.shape[0], tm))


def kernel(x, norm_g, ffn1_wg, ffn1_wu, ffn1_wd, ffn2_wg, ffn2_wu, ffn2_wd,
           diff_w_in, diff_w_out, diff_lambda, diff_subln,
           na_w_in, na_w_out, na_rpb,
           gqa_w_in, gqa_w_out, gqa_qk_norm):
    b, s, d = x.shape
    n = b * s
    tm = 512
    h = x.reshape(n, d)
    ones_gain = jnp.ones((HEAD_DIM, tm), F32)

    for i in range(DEPTH):
        g = norm_g[i]
        h = _ffn(h, g[0:2], ffn1_wg[i].astype(BF16), ffn1_wu[i].astype(BF16), ffn1_wd[i].astype(BF16))
        kind, j = i % N_MIXERS, i // N_MIXERS
        h3 = h.reshape(b, s, d)
        if kind == 0:
            w = diff_w_in[j]
            wqT = w[:, 0:d].T.astype(BF16)
            wkT = w[:, d:2 * d].T.astype(BF16)
            wvT = w[:, 2 * d:3 * d].T.astype(BF16)
            ct, st = _diff_tables(s)
            qT, k, vT = _inproj_t(h3, g[2:3], wqT, wkT, wvT, ct, st, ones_gain, ones_gain, "diff", tm)
            lam_init = 0.8 - 0.6 * math.exp(-0.3 * i)
            gsub_b = _col_gain(diff_subln[j], _FLASH_LANES // 2)
            o = _flash(qT, k, vT, "diff", diff_lambda[j].astype(F32), gsub_b, lam_init)
            w_out = diff_w_out[j]
        elif kind == 1:
            q, k, v = _inproj_n(h3, g[2:3], na_w_in[j].astype(BF16), tm)
            rpb = na_rpb[j].astype(F32)
            rpb_pad = jnp.zeros((rpb.shape[0], 16, 128), F32).at[:, :rpb.shape[1], :rpb.shape[2]].set(rpb)
            pt = _na_table(rpb_pad)
            o = _na_attention(q, k, v, pt)
            w_out = na_w_out[j]
        else:
            w = gqa_w_in[j]
            nq = d
            nkv = (w.shape[1] - nq) // 2
            wqT = w[:, 0:nq].T.astype(BF16)
            wkT = w[:, nq:nq + nkv].T.astype(BF16)
            wvT = w[:, nq + nkv:].T.astype(BF16)
            ct, st = _axial_tables(s)
            gq_b = _col_gain(gqa_qk_norm[j, 0], tm)
            gk_b = _col_gain(gqa_qk_norm[j, 1], tm)
            qT, k, vT = _inproj_t(h3, g[2:3], wqT, wkT, wvT, ct, st, gq_b, gk_b, "gqa", tm)
            o = _flash(qT, k, vT, "gqa")
            w_out = gqa_w_out[j]
        h = _outproj(o.reshape(n, d), h, g[3:4], w_out.astype(BF16), tm)
        h = _ffn(h, g[4:6], ffn2_wg[i].astype(BF16), ffn2_wu[i].astype(BF16), ffn2_wd[i].astype(BF16))
    return h.reshape(b, s, d)
```

```python
import functools
import math

import jax
import jax.numpy as jnp
from jax import lax
from jax.experimental import pallas as pl
from jax.experimental.pallas import tpu as pltpu

D_MODEL = 1024
SEQ = 8192
DEPTH = 4
N_MIXERS = 3
GRID_W = 64
EPS = 1e-6
D_FF = 2816
FFN_RES = 0.5
HEAD_DIM = 64
ROPE_THETA = 500000.0
ROPE_DIMS = 16
NA_WIN_H = 8
NA_WIN_W = 16
AXIAL_THETA = 10000.0

LOG2E = 1.4426950408889634
QSCALE = (HEAD_DIM ** -0.5) * LOG2E
NEG = -1e30

VMEM_LIMIT_BYTES = 56 * 1024 * 1024

BF16 = jnp.bfloat16
F32 = jnp.float32

_NT = (((1,), (1,)), ((), ()))


def _cparams(n_axes):
    return pltpu.CompilerParams(
        dimension_semantics=("parallel",) * n_axes,
        vmem_limit_bytes=VMEM_LIMIT_BYTES,
    )


def _resident(shape, index_map):
    return pl.BlockSpec(shape, index_map, pipeline_mode=pl.Buffered(1))


def _rms_rows(x, g):
    ms = jnp.mean(x * x, axis=-1, keepdims=True)
    return x * lax.rsqrt(ms + EPS) * g


def _ffn_block(x, g_pre, g_post, wg_ref, wu_ref, wd_ref):
    xn = _rms_rows(x, g_pre).astype(BF16)
    gate = jnp.dot(xn, wg_ref[...], preferred_element_type=F32)
    up = jnp.dot(xn, wu_ref[...], preferred_element_type=F32)
    act = (gate * jax.nn.sigmoid(gate) * up).astype(BF16)
    y = jnp.dot(act, wd_ref[...], preferred_element_type=F32)
    return x + FFN_RES * _rms_rows(y, g_post)


def _ffn_kernel(x_ref, g_ref, wg_ref, wu_ref, wd_ref, o_ref):
    g = g_ref[...]
    o_ref[...] = _ffn_block(x_ref[...], g[0:1], g[1:2], wg_ref, wu_ref, wd_ref)


def _outproj_ffn_kernel(o_ref, h_ref, g_ref, wo_ref, wg_ref, wu_ref, wd_ref, out_ref):
    g = g_ref[...]
    m = jnp.dot(o_ref[...], wo_ref[...], preferred_element_type=F32)
    h1 = h_ref[...] + _rms_rows(m, g[0:1])
    out_ref[...] = _ffn_block(h1, g[1:2], g[2:3], wg_ref, wu_ref, wd_ref)


def _outproj_ffn(o2d, h2d, g3, wo, wg, wu, wd, tm=512):
    n = h2d.shape[0]
    blk = pl.BlockSpec((tm, D_MODEL), lambda i: (i, 0))
    return pl.pallas_call(
        _outproj_ffn_kernel,
        grid=(n // tm,),
        in_specs=[
            blk, blk,
            _resident((3, D_MODEL), lambda i: (0, 0)),
            _resident((D_MODEL, D_MODEL), lambda i: (0, 0)),
            _resident((D_MODEL, D_FF), lambda i: (0, 0)),
            _resident((D_MODEL, D_FF), lambda i: (0, 0)),
            _resident((D_FF, D_MODEL), lambda i: (0, 0)),
        ],
        out_specs=blk,
        out_shape=jax.ShapeDtypeStruct(h2d.shape, F32),
        compiler_params=_cparams(1),
        name="outproj_ffn",
    )(o2d, h2d, g3, wo, wg, wu, wd)


def _ffn(h2d, g2, wg, wu, wd, tm=512):
    n = h2d.shape[0]
    return pl.pallas_call(
        _ffn_kernel,
        grid=(n // tm,),
        in_specs=[
            pl.BlockSpec((tm, D_MODEL), lambda i: (i, 0)),
            _resident((2, D_MODEL), lambda i: (0, 0)),
            _resident((D_MODEL, D_FF), lambda i: (0, 0)),
            _resident((D_MODEL, D_FF), lambda i: (0, 0)),
            _resident((D_FF, D_MODEL), lambda i: (0, 0)),
        ],
        out_specs=pl.BlockSpec((tm, D_MODEL), lambda i: (i, 0)),
        out_shape=jax.ShapeDtypeStruct(h2d.shape, F32),
        compiler_params=_cparams(1),
        name="ffn",
    )(h2d, g2, wg, wu, wd)


def _swap_halves(x, half):
    blocks = []
    r = 0
    n = x.shape[0]
    while r + 2 * half <= n:
        blocks += [x[r + half:r + 2 * half], x[r:r + half]]
        r += 2 * half
    return jnp.concatenate(blocks, axis=0)


def _inproj_t_kernel(h_ref, g_ref, wqT_ref, wkT_ref, wvT_ref, ct_ref, st_ref, gq_ref, gk_ref,
                     qT_ref, k_ref, vT_ref, *, variant):
    u = _rms_rows(h_ref[0], g_ref[...]).astype(BF16)
    ct = ct_ref[...]
    st = st_ref[...]

    def head(x, gain, scale):
        if variant == "gqa":
            ms = jnp.mean(x * x, axis=0, keepdims=True)
            x = x * lax.rsqrt(ms + EPS) * gain
            xs = _swap_halves(x, 16)
        else:
            xs = jnp.concatenate([x[8:16], x[0:8], x[16:64]], axis=0)
        x = x * ct + xs * st
        if scale is not None:
            x = x * scale
        return x

    gq = gq_ref[...]
    gk = gk_ref[...]

    qT = lax.dot_general(wqT_ref[...], u, _NT, preferred_element_type=F32)
    for hh in range(qT.shape[0] // HEAD_DIM):
        sl = slice(hh * HEAD_DIM, (hh + 1) * HEAD_DIM)
        qT_ref[0, sl, :] = head(qT[sl], gq, QSCALE).astype(BF16)

    kT = lax.dot_general(wkT_ref[...], u, _NT, preferred_element_type=F32)
    for pp in range(kT.shape[0] // 128):
        pair = jnp.concatenate(
            [head(kT[pp * 128 + e * 64: pp * 128 + (e + 1) * 64], gk, None) for e in range(2)], axis=0)
        k_ref[0, :, pp * 128:(pp + 1) * 128] = pair.T.astype(BF16)

    vT = lax.dot_general(wvT_ref[...], u, _NT, preferred_element_type=F32)
    vT_ref[0] = vT.astype(BF16)


def _inproj_t(h3d, g, wqT, wkT, wvT, ct, st, gq_b, gk_b, variant, tm=512):
    b, s, _ = h3d.shape
    nq, nk, nv = wqT.shape[0], wkT.shape[0], wvT.shape[0]
    kern = functools.partial(_inproj_t_kernel, variant=variant)
    return pl.pallas_call(
        kern,
        grid=(b, s // tm),
        in_specs=[
            pl.BlockSpec((1, tm, D_MODEL), lambda bi, ti: (bi, ti, 0)),
            _resident((1, D_MODEL), lambda bi, ti: (0, 0)),
            _resident((nq, D_MODEL), lambda bi, ti: (0, 0)),
            _resident((nk, D_MODEL), lambda bi, ti: (0, 0)),
            _resident((nv, D_MODEL), lambda bi, ti: (0, 0)),
            pl.BlockSpec((HEAD_DIM, tm), lambda bi, ti: (0, ti)),
            pl.BlockSpec((HEAD_DIM, tm), lambda bi, ti: (0, ti)),
            _resident((HEAD_DIM, tm), lambda bi, ti: (0, 0)),
            _resident((HEAD_DIM, tm), lambda bi, ti: (0, 0)),
        ],
        out_specs=[
            pl.BlockSpec((1, nq, tm), lambda bi, ti: (bi, 0, ti)),
            pl.BlockSpec((1, tm, nk), lambda bi, ti: (bi, ti, 0)),
            pl.BlockSpec((1, nv, tm), lambda bi, ti: (bi, 0, ti)),
        ],
        out_shape=[
            jax.ShapeDtypeStruct((b, nq, s), BF16),
            jax.ShapeDtypeStruct((b, s, nk), BF16),
            jax.ShapeDtypeStruct((b, nv, s), BF16),
        ],
        compiler_params=_cparams(2),
        name="inproj_" + variant,
    )(h3d, g, wqT, wkT, wvT, ct, st, gq_b, gk_b)


def _inproj_n_kernel(h_ref, g_ref, w_ref, q_ref, k_ref, v_ref):
    u = _rms_rows(h_ref[0], g_ref[...]).astype(BF16)
    y = jnp.dot(u, w_ref[...], preferred_element_type=F32)
    q_ref[0] = (y[:, 0:D_MODEL] * QSCALE).astype(BF16)
    k_ref[0] = y[:, D_MODEL:2 * D_MODEL].astype(BF16)
    v_ref[0] = y[:, 2 * D_MODEL:3 * D_MODEL].astype(BF16)


def _inproj_n(h3d, g, w, tm=512):
    b, s, _ = h3d.shape
    blk = pl.BlockSpec((1, tm, D_MODEL), lambda bi, ti: (bi, ti, 0))
    return pl.pallas_call(
        _inproj_n_kernel,
        grid=(b, s // tm),
        in_specs=[
            blk,
            _resident((1, D_MODEL), lambda bi, ti: (0, 0)),
            _resident((D_MODEL, 3 * D_MODEL), lambda bi, ti: (0, 0)),
        ],
        out_specs=[blk, blk, blk],
        out_shape=[jax.ShapeDtypeStruct((b, s, D_MODEL), BF16)] * 3,
        compiler_params=_cparams(2),
        name="inproj_na",
    )(h3d, g, w)


_SUM_ROWS = 16
_RING = 4
_LOOKAHEAD = 3
_FLASH_LANES = 1024


def _flash_kernel(qT_ref, k_ref, vT_ref, *rest, variant, qb, kb, nk, unroll, lam_init):
    if variant == "diff":
        lam_ref, gsub_ref, o_ref, s_ring, p_ring, acc_sc = rest
    else:
        o_ref, s_ring, p_ring, acc_sc = rest

    qT = qT_ref[0]
    zero = jnp.zeros((HEAD_DIM, qb), BF16)
    if variant == "diff":
        qz = jnp.concatenate(
            [jnp.concatenate([qT[0:64], zero], axis=0),
             jnp.concatenate([zero, qT[64:128]], axis=0)], axis=1)
    else:
        parity = pl.program_id(1) % 2
        row_half = (lax.broadcasted_iota(jnp.int32, (128, qb), 0) >= HEAD_DIM).astype(jnp.int32)
        keep = row_half == parity
        parts = []
        for g in range(4):
            qg = qT[g * 64:(g + 1) * 64]
            parts.append(jnp.where(keep, jnp.concatenate([qg, qg], axis=0), jnp.zeros((), BF16)))
        qz = jnp.concatenate(parts, axis=1)

    lanes = qz.shape[1]

    def scores(i):
        ks = pl.multiple_of(i * kb, kb)
        return jnp.dot(k_ref[0, pl.ds(ks, kb), :], qz, preferred_element_type=F32)

    dv = vT_ref.shape[1]
    ones_rows = jnp.ones((_SUM_ROWS, kb), BF16)

    def weighted_values(i, p):
        ks = pl.multiple_of(i * kb, kb)
        v_ext = jnp.concatenate([vT_ref[0, :, pl.ds(ks, kb)], ones_rows], axis=0)
        return jnp.dot(v_ext, p, preferred_element_type=F32)

    def stage(i, slot, stats, with_scores=True):
        alpha_prev, m_prev, maxes = stats
        m_new = jnp.maximum(m_prev, maxes[0])
        if with_scores:
            s_new = scores(i + _LOOKAHEAD)
            s_ring[(slot + _LOOKAHEAD) % _RING] = s_new
            maxes = maxes[1:] + (jnp.max(s_new, axis=0, keepdims=True),)
        else:
            maxes = maxes[1:] + (maxes[0],)
        acc_sc[...] = alpha_prev * acc_sc[...] + weighted_values(jnp.maximum(i - 1, 0), p_ring[(slot + 1) % 2])
        p_ring[slot % 2] = jnp.exp2(s_ring[slot % _RING] - m_new).astype(BF16)
        return jnp.exp2(m_prev - m_new), m_new, maxes

    def body(ii, stats):
        for u in range(unroll):
            stats = stage(unroll * ii + u, u, stats)
        return stats

    first = [scores(t) for t in range(_LOOKAHEAD)]
    for t in range(_LOOKAHEAD):
        s_ring[t] = first[t]
    p_ring[1] = jnp.zeros(p_ring.shape[1:], BF16)
    acc_sc[...] = jnp.zeros(acc_sc.shape, F32)
    stats = (jnp.ones((1, lanes), F32), jnp.full((1, lanes), NEG, F32),
             tuple(jnp.max(f, axis=0, keepdims=True) for f in first))
    n_body = nk // unroll
    stats = lax.fori_loop(0, n_body - 1, body, stats)
    for u in range(unroll):
        i_last = unroll * (n_body - 1) + u
        stats = stage(i_last, u, stats, with_scores=i_last + _LOOKAHEAD < nk)
    acc = stats[0] * acc_sc[...] + weighted_values(nk - 1, p_ring[(nk - 1) % 2])

    o = acc[0:dv] / acc[dv:dv + 1]
    if variant == "diff":
        lv = lam_ref[...]
        lam = (jnp.exp(jnp.sum(lv[0:1] * lv[1:2], axis=1, keepdims=True))
               - jnp.exp(jnp.sum(lv[2:3] * lv[3:4], axis=1, keepdims=True)) + lam_init)
        od = o[:, 0:qb] - lam * o[:, qb:2 * qb]
        ms = jnp.mean(od * od, axis=0, keepdims=True)
        od = od * lax.rsqrt(ms + EPS) * gsub_ref[...] * (1.0 - lam_init)
        o_ref[0] = od.T.astype(BF16)
    else:
        stacked = jnp.concatenate([o[:, g * qb:(g + 1) * qb] for g in range(4)], axis=0)
        o_ref[0] = stacked.T.astype(BF16)


def _flash(qT, k, vT, variant, lam_vecs=None, gsub_b=None, lam_init=0.0, kb=256, unroll=8):
    b, _, s = qT.shape
    lanes = _FLASH_LANES
    if variant == "diff":
        qb, n_groups, r, dv, width = lanes // 2, 8, 128, 128, 128
        k_map = lambda bi, j, qi: (bi, 0, j)
    else:
        qb, n_groups, r, dv, width = lanes // 4, 4, 256, 64, 256
        k_map = lambda bi, j, qi: (bi, 0, j // 2)
    nk = s // kb
    assert unroll % _RING == 0 and nk % unroll == 0
    kern = functools.partial(_flash_kernel, variant=variant, qb=qb, kb=kb, nk=nk, unroll=unroll,
                             lam_init=lam_init)
    in_specs = [
        pl.BlockSpec((1, r, qb), lambda bi, j, qi: (bi, j, qi)),
        pl.BlockSpec((1, s, 128), k_map),
        pl.BlockSpec((1, dv, s), lambda bi, j, qi: (bi, j, 0)),
    ]
    args = [qT, k, vT]
    if variant == "diff":
        in_specs += [
            _resident((4, HEAD_DIM), lambda bi, j, qi: (0, 0)),
            _resident((128, qb), lambda bi, j, qi: (0, 0)),
        ]
        args += [lam_vecs, gsub_b]
    return pl.pallas_call(
        kern,
        grid=(b, n_groups, s // qb),
        in_specs=in_specs,
        out_specs=pl.BlockSpec((1, qb, width), lambda bi, j, qi: (bi, qi, j)),
        out_shape=jax.ShapeDtypeStruct((b, s, D_MODEL), BF16),
        scratch_shapes=[
            pltpu.VMEM((_RING, kb, lanes), F32),
            pltpu.VMEM((2, kb, lanes), BF16),
            pltpu.VMEM((dv + _SUM_ROWS, lanes), F32),
        ],
        compiler_params=_cparams(3),
        name="flash_" + variant,
    )(*args)


def _na_table_kernel(rpb_ref, pt_ref):
    cq = lax.broadcasted_iota(jnp.int32, (GRID_W, 128), 0)
    ck = lax.broadcasted_iota(jnp.int32, (GRID_W, 128), 1) & (GRID_W - 1)
    c0 = jnp.clip(cq - NA_WIN_W // 2, 0, GRID_W - NA_WIN_W)
    valid = (ck >= c0) & (ck < c0 + NA_WIN_W)
    for dr in range(2 * NA_WIN_H - 2):
        a = rpb_ref[0, dr:dr + 1, :]
        bb = rpb_ref[0, dr + 1:dr + 2, :]
        x = a + pltpu.roll(bb, GRID_W, axis=1)
        xb = jnp.broadcast_to(x, (GRID_W, 128))
        t = pltpu.roll(xb, 128 - (NA_WIN_W - 1), axis=1, stride=1, stride_axis=0)
        pt_ref[0, dr] = jnp.where(valid, t * LOG2E, NEG)


def _na_table(rpb_pad):
    nh = rpb_pad.shape[0]
    ndr = 2 * NA_WIN_H - 2
    return pl.pallas_call(
        _na_table_kernel,
        grid=(nh,),
        in_specs=[pl.BlockSpec((1, 16, 128), lambda h: (h, 0, 0))],
        out_specs=pl.BlockSpec((1, ndr, GRID_W, 128), lambda h: (h, 0, 0, 0)),
        out_shape=jax.ShapeDtypeStruct((nh, ndr, GRID_W, 128), F32),
        compiler_params=_cparams(1),
        name="na_table",
    )(rpb_pad)


_NA_ROWS = 4
_NA_BAND = _NA_ROWS + NA_WIN_H


def _na_kernel(q_ref, k0_ref, k1_ref, k2_ref, v0_ref, v1_ref, v2_ref, pt_ref, o_ref, kband, vband,
               *, n_steps):
    i = pl.program_id(1)
    blk = _NA_ROWS * GRID_W
    for n, (kr, vr) in enumerate(((k0_ref, v0_ref), (k1_ref, v1_ref), (k2_ref, v2_ref))):
        kband[n * blk:(n + 1) * blk, :] = kr[0]
        vband[n * blk:(n + 1) * blk, :] = vr[0]
    left = lax.broadcasted_iota(jnp.int32, (GRID_W, 128), 1) < HEAD_DIM
    nkeys = NA_WIN_H * GRID_W

    def row_body(t, carry):
        off = jnp.where(i == 0, 0, jnp.where(i == n_steps - 1, _NA_ROWS, t))
        sidx = jnp.where(i == 0, NA_WIN_H - 1 - t, jnp.where(i == n_steps - 1, 3 - t, 3))
        start = pl.multiple_of(off * GRID_W, GRID_W)
        qs = pl.multiple_of(t * GRID_W, GRID_W)
        n_pairs = D_MODEL // 128
        cols = [slice(j * 128, (j + 1) * 128) for j in range(n_pairs)]
        scores = []
        for j in range(n_pairs):
            qp = q_ref[0, pl.ds(qs, GRID_W), cols[j]]
            kp = kband[pl.ds(start, nkeys), cols[j]]
            for e in range(2):
                qz = jnp.where(left if e == 0 else jnp.logical_not(left), qp, jnp.zeros((), BF16))
                scores.append(lax.dot_general(qz, kp, _NT, preferred_element_type=F32))
        probs, denoms = [], []
        for h in range(2 * n_pairs):
            bias = jnp.concatenate([pt_ref[h, sidx + 2 * m] for m in range(NA_WIN_H // 2)], axis=1)
            s = scores[h] + bias
            p = jnp.exp2(s - jnp.max(s, axis=1, keepdims=True))
            denoms.append(jnp.sum(p, axis=1, keepdims=True))
            probs.append(p.astype(BF16))
        for j in range(n_pairs):
            vp = vband[pl.ds(start, nkeys), cols[j]]
            outs = [jnp.dot(probs[2 * j + e], vp, preferred_element_type=F32) / denoms[2 * j + e]
                    for e in range(2)]
            o_ref[0, pl.ds(qs, GRID_W), cols[j]] = jnp.where(left, outs[0], outs[1]).astype(BF16)
        return carry

    lax.fori_loop(0, _NA_ROWS, row_body, 0)


def _na_attention(q, k, v, pt):
    b, s, _ = q.shape
    blk = _NA_ROWS * GRID_W
    n_steps = s // blk
    last = n_steps - 3

    def band_spec(n):
        return pl.BlockSpec((1, blk, D_MODEL),
                            lambda bi, i: (bi, jnp.clip(i - 1, 0, last) + n, 0))

    kern = functools.partial(_na_kernel, n_steps=n_steps)
    return pl.pallas_call(
        kern,
        grid=(b, n_steps),
        in_specs=[
            pl.BlockSpec((1, blk, D_MODEL), lambda bi, i: (bi, i, 0)),
            band_spec(0), band_spec(1), band_spec(2),
            band_spec(0), band_spec(1), band_spec(2),
            _resident(pt.shape, lambda bi, i: (0, 0, 0, 0)),
        ],
        out_specs=pl.BlockSpec((1, blk, D_MODEL), lambda bi, i: (bi, i, 0)),
        out_shape=jax.ShapeDtypeStruct((b, s, D_MODEL), BF16),
        scratch_shapes=[
            pltpu.VMEM((_NA_BAND * GRID_W, D_MODEL), BF16),
            pltpu.VMEM((_NA_BAND * GRID_W, D_MODEL), BF16),
        ],
        compiler_params=_cparams(2),
        name="na_attention",
    )(q, k, k, k, v, v, v, pt)


def _angles(pos, dims, theta):
    inv = theta ** (-jnp.arange(0, dims, 2, dtype=F32) / dims)
    return pos[:, None] * inv[None, :]


def _diff_tables(s):
    ang = _angles(jnp.arange(s, dtype=F32), ROPE_DIMS, ROPE_THETA)
    c, sn = jnp.cos(ang).T, jnp.sin(ang).T
    ones = jnp.ones((HEAD_DIM - ROPE_DIMS, s), F32)
    ct = jnp.concatenate([c, c, ones], axis=0)
    st = jnp.concatenate([-sn, sn, 0.0 * ones], axis=0)
    return ct, st


def _axial_tables(s):
    t = jnp.arange(s)
    half = HEAD_DIM // 2
    ar = _angles((t // GRID_W).astype(F32), half, AXIAL_THETA)
    ac = _angles((t % GRID_W).astype(F32), half, AXIAL_THETA)
    cr, sr, cc, sc = jnp.cos(ar).T, jnp.sin(ar).T, jnp.cos(ac).T, jnp.sin(ac).T
    ct = jnp.concatenate([cr, cr, cc, cc], axis=0)
    st = jnp.concatenate([-sr, sr, -sc, sc], axis=0)
    return ct, st


def _col_gain(g, tm):
    return jnp.broadcast_to(g.astype(F32)[:, None], (g.shape[0], tm))


def kernel(x, norm_g, ffn1_wg, ffn1_wu, ffn1_wd, ffn2_wg, ffn2_wu, ffn2_wd,
           diff_w_in, diff_w_out, diff_lambda, diff_subln,
           na_w_in, na_w_out, na_rpb,
           gqa_w_in, gqa_w_out, gqa_qk_norm):
    b, s, d = x.shape
    n = b * s
    tm = 512
    h = x.reshape(n, d)
    ones_gain = jnp.ones((HEAD_DIM, tm), F32)

    for i in range(DEPTH):
        g = norm_g[i]
        h = _ffn(h, g[0:2], ffn1_wg[i].astype(BF16), ffn1_wu[i].astype(BF16), ffn1_wd[i].astype(BF16))
        kind, j = i % N_MIXERS, i // N_MIXERS
        h3 = h.reshape(b, s, d)
        if kind == 0:
            w = diff_w_in[j]
            wqT = w[:, 0:d].T.astype(BF16)
            wkT = w[:, d:2 * d].T.astype(BF16)
            wvT = w[:, 2 * d:3 * d].T.astype(BF16)
            ct, st = _diff_tables(s)
            qT, k, vT = _inproj_t(h3, g[2:3], wqT, wkT, wvT, ct, st, ones_gain, ones_gain, "diff", tm)
            lam_init = 0.8 - 0.6 * math.exp(-0.3 * i)
            gsub_b = _col_gain(diff_subln[j], _FLASH_LANES // 2)
            o = _flash(qT, k, vT, "diff", diff_lambda[j].astype(F32), gsub_b, lam_init)
            w_out = diff_w_out[j]
        elif kind == 1:
            q, k, v = _inproj_n(h3, g[2:3], na_w_in[j].astype(BF16), tm)
            rpb = na_rpb[j].astype(F32)
            rpb_pad = jnp.zeros((rpb.shape[0], 16, 128), F32).at[:, :rpb.shape[1], :rpb.shape[2]].set(rpb)
            pt = _na_table(rpb_pad)
            o = _na_attention(q, k, v, pt)
            w_out = na_w_out[j]
        else:
            w = gqa_w_in[j]
            nq = d
            nkv = (w.shape[1] - nq) // 2
            wqT = w[:, 0:nq].T.astype(BF16)
            wkT = w[:, nq:nq + nkv].T.astype(BF16)
            wvT = w[:, nq + nkv:].T.astype(BF16)
            ct, st = _axial_tables(s)
            gq_b = _col_gain(gqa_qk_norm[j, 0], tm)
            gk_b = _col_gain(gqa_qk_norm[j, 1], tm)
            qT, k, vT = _inproj_t(h3, g[2:3], wqT, wkT, wvT, ct, st, gq_b, gk_b, "gqa", tm)
            o = _flash(qT, k, vT, "gqa")
            w_out = gqa_w_out[j]
        h = _outproj_ffn(o.reshape(n, d), h, g[3:6], w_out.astype(BF16),
                         ffn2_wg[i].astype(BF16), ffn2_wu[i].astype(BF16), ffn2_wd[i].astype(BF16), tm)
    return h.reshape(b, s, d)
```

```python
import functools
import math

import jax
import jax.numpy as jnp
from jax import lax
from jax.experimental import pallas as pl
from jax.experimental.pallas import tpu as pltpu

D_MODEL = 1024
SEQ = 8192
DEPTH = 4
N_MIXERS = 3
GRID_W = 64
EPS = 1e-6
D_FF = 2816
FFN_RES = 0.5
HEAD_DIM = 64
ROPE_THETA = 500000.0
ROPE_DIMS = 16
NA_WIN_H = 8
NA_WIN_W = 16
AXIAL_THETA = 10000.0

LOG2E = 1.4426950408889634
QSCALE = (HEAD_DIM ** -0.5) * LOG2E
NEG = -1e30

VMEM_LIMIT_BYTES = 56 * 1024 * 1024

BF16 = jnp.bfloat16
F32 = jnp.float32

_NT = (((1,), (1,)), ((), ()))


def _cparams(n_axes):
    return pltpu.CompilerParams(
        dimension_semantics=("parallel",) * n_axes,
        vmem_limit_bytes=VMEM_LIMIT_BYTES,
    )


def _resident(shape, index_map):
    return pl.BlockSpec(shape, index_map, pipeline_mode=pl.Buffered(1))


def _rms_rows(x, g):
    ms = jnp.mean(x * x, axis=-1, keepdims=True)
    return x * lax.rsqrt(ms + EPS) * g


def _ffn_block(x, g_pre, g_post, wg_ref, wu_ref, wd_ref):
    xn = _rms_rows(x, g_pre).astype(BF16)
    gate = jnp.dot(xn, wg_ref[...], preferred_element_type=F32)
    up = jnp.dot(xn, wu_ref[...], preferred_element_type=F32)
    act = (gate * jax.nn.sigmoid(gate) * up).astype(BF16)
    y = jnp.dot(act, wd_ref[...], preferred_element_type=F32)
    return x + FFN_RES * _rms_rows(y, g_post)


def _ffn_kernel(x_ref, g_ref, wg_ref, wu_ref, wd_ref, o_ref):
    g = g_ref[...]
    o_ref[...] = _ffn_block(x_ref[...], g[0:1], g[1:2], wg_ref, wu_ref, wd_ref)


def _outproj_ffn_kernel(o_ref, h_ref, g_ref, wo_ref, wg_ref, wu_ref, wd_ref, out_ref):
    g = g_ref[...]
    m = jnp.dot(o_ref[...], wo_ref[...], preferred_element_type=F32)
    h1 = h_ref[...] + _rms_rows(m, g[0:1])
    out_ref[...] = _ffn_block(h1, g[1:2], g[2:3], wg_ref, wu_ref, wd_ref)


def _outproj_ffn(o2d, h2d, g3, wo, wg, wu, wd, tm=512):
    n = h2d.shape[0]
    blk = pl.BlockSpec((tm, D_MODEL), lambda i: (i, 0))
    return pl.pallas_call(
        _outproj_ffn_kernel,
        grid=(n // tm,),
        in_specs=[
            blk, blk,
            _resident((3, D_MODEL), lambda i: (0, 0)),
            _resident((D_MODEL, D_MODEL), lambda i: (0, 0)),
            _resident((D_MODEL, D_FF), lambda i: (0, 0)),
            _resident((D_MODEL, D_FF), lambda i: (0, 0)),
            _resident((D_FF, D_MODEL), lambda i: (0, 0)),
        ],
        out_specs=blk,
        out_shape=jax.ShapeDtypeStruct(h2d.shape, F32),
        compiler_params=_cparams(1),
        name="outproj_ffn",
    )(o2d, h2d, g3, wo, wg, wu, wd)


def _ffn(h2d, g2, wg, wu, wd, tm=512):
    n = h2d.shape[0]
    return pl.pallas_call(
        _ffn_kernel,
        grid=(n // tm,),
        in_specs=[
            pl.BlockSpec((tm, D_MODEL), lambda i: (i, 0)),
            _resident((2, D_MODEL), lambda i: (0, 0)),
            _resident((D_MODEL, D_FF), lambda i: (0, 0)),
            _resident((D_MODEL, D_FF), lambda i: (0, 0)),
            _resident((D_FF, D_MODEL), lambda i: (0, 0)),
        ],
        out_specs=pl.BlockSpec((tm, D_MODEL), lambda i: (i, 0)),
        out_shape=jax.ShapeDtypeStruct(h2d.shape, F32),
        compiler_params=_cparams(1),
        name="ffn",
    )(h2d, g2, wg, wu, wd)


def _swap_halves(x, half):
    blocks = []
    r = 0
    n = x.shape[0]
    while r + 2 * half <= n:
        blocks += [x[r + half:r + 2 * half], x[r:r + half]]
        r += 2 * half
    return jnp.concatenate(blocks, axis=0)


def _inproj_t_kernel(h_ref, g_ref, wqT_ref, wkT_ref, wvT_ref, ct_ref, st_ref, gq_ref, gk_ref,
                     qT_ref, k_ref, vT_ref, *, variant):
    u = _rms_rows(h_ref[0], g_ref[...]).astype(BF16)
    ct = ct_ref[...]
    st = st_ref[...]

    def head(x, gain, scale):
        if variant == "gqa":
            ms = jnp.mean(x * x, axis=0, keepdims=True)
            x = x * lax.rsqrt(ms + EPS) * gain
            xs = _swap_halves(x, 16)
        else:
            xs = jnp.concatenate([x[8:16], x[0:8], x[16:64]], axis=0)
        x = x * ct + xs * st
        if scale is not None:
            x = x * scale
        return x

    gq = gq_ref[...]
    gk = gk_ref[...]

    qT = lax.dot_general(wqT_ref[...], u, _NT, preferred_element_type=F32)
    for hh in range(qT.shape[0] // HEAD_DIM):
        sl = slice(hh * HEAD_DIM, (hh + 1) * HEAD_DIM)
        qT_ref[0, sl, :] = head(qT[sl], gq, QSCALE).astype(BF16)

    kT = lax.dot_general(wkT_ref[...], u, _NT, preferred_element_type=F32)
    for pp in range(kT.shape[0] // 128):
        pair = jnp.concatenate(
            [head(kT[pp * 128 + e * 64: pp * 128 + (e + 1) * 64], gk, None) for e in range(2)], axis=0)
        k_ref[0, :, pp * 128:(pp + 1) * 128] = pair.T.astype(BF16)

    vT = lax.dot_general(wvT_ref[...], u, _NT, preferred_element_type=F32)
    vT_ref[0] = vT.astype(BF16)


def _inproj_t(h3d, g, wqT, wkT, wvT, ct, st, gq_b, gk_b, variant, tm=512):
    b, s, _ = h3d.shape
    nq, nk, nv = wqT.shape[0], wkT.shape[0], wvT.shape[0]
    kern = functools.partial(_inproj_t_kernel, variant=variant)
    return pl.pallas_call(
        kern,
        grid=(b, s // tm),
        in_specs=[
            pl.BlockSpec((1, tm, D_MODEL), lambda bi, ti: (bi, ti, 0)),
            _resident((1, D_MODEL), lambda bi, ti: (0, 0)),
            _resident((nq, D_MODEL), lambda bi, ti: (0, 0)),
            _resident((nk, D_MODEL), lambda bi, ti: (0, 0)),
            _resident((nv, D_MODEL), lambda bi, ti: (0, 0)),
            pl.BlockSpec((HEAD_DIM, tm), lambda bi, ti: (0, ti)),
            pl.BlockSpec((HEAD_DIM, tm), lambda bi, ti: (0, ti)),
            _resident((HEAD_DIM, tm), lambda bi, ti: (0, 0)),
            _resident((HEAD_DIM, tm), lambda bi, ti: (0, 0)),
        ],
        out_specs=[
            pl.BlockSpec((1, nq, tm), lambda bi, ti: (bi, 0, ti)),
            pl.BlockSpec((1, tm, nk), lambda bi, ti: (bi, ti, 0)),
            pl.BlockSpec((1, nv, tm), lambda bi, ti: (bi, 0, ti)),
        ],
        out_shape=[
            jax.ShapeDtypeStruct((b, nq, s), BF16),
            jax.ShapeDtypeStruct((b, s, nk), BF16),
            jax.ShapeDtypeStruct((b, nv, s), BF16),
        ],
        compiler_params=_cparams(2),
        name="inproj_" + variant,
    )(h3d, g, wqT, wkT, wvT, ct, st, gq_b, gk_b)


def _inproj_n_kernel(h_ref, g_ref, w_ref, q_ref, k_ref, v_ref):
    u = _rms_rows(h_ref[0], g_ref[...]).astype(BF16)
    y = jnp.dot(u, w_ref[...], preferred_element_type=F32)
    q_ref[0] = (y[:, 0:D_MODEL] * QSCALE).astype(BF16)
    k_ref[0] = y[:, D_MODEL:2 * D_MODEL].astype(BF16)
    v_ref[0] = y[:, 2 * D_MODEL:3 * D_MODEL].astype(BF16)


def _inproj_n(h3d, g, w, tm=512):
    b, s, _ = h3d.shape
    blk = pl.BlockSpec((1, tm, D_MODEL), lambda bi, ti: (bi, ti, 0))
    return pl.pallas_call(
        _inproj_n_kernel,
        grid=(b, s // tm),
        in_specs=[
            blk,
            _resident((1, D_MODEL), lambda bi, ti: (0, 0)),
            _resident((D_MODEL, 3 * D_MODEL), lambda bi, ti: (0, 0)),
        ],
        out_specs=[blk, blk, blk],
        out_shape=[jax.ShapeDtypeStruct((b, s, D_MODEL), BF16)] * 3,
        compiler_params=_cparams(2),
        name="inproj_na",
    )(h3d, g, w)


_SUM_ROWS = 16
_RING = 4
_LOOKAHEAD = 3
_FLASH_LANES = 1024


def _flash_kernel(qT_ref, k_ref, vT_ref, *rest, variant, qb, nq, kb, nk, unroll, lam_init):
    if variant == "diff":
        lam_ref, gsub_ref, o_ref, s_ring, p_ring, acc_sc = rest
    else:
        o_ref, s_ring, p_ring, acc_sc = rest

    def padded_queries(qq):
        qT = qT_ref[0, :, pl.ds(pl.multiple_of(qq * qb, qb), qb)]
        if variant == "diff":
            zero = jnp.zeros((HEAD_DIM, qb), BF16)
            return jnp.concatenate(
                [jnp.concatenate([qT[0:64], zero], axis=0),
                 jnp.concatenate([zero, qT[64:128]], axis=0)], axis=1)
        parity = pl.program_id(1) % 2
        row_half = (lax.broadcasted_iota(jnp.int32, (128, qb), 0) >= HEAD_DIM).astype(jnp.int32)
        keep = row_half == parity
        parts = []
        for g in range(4):
            qg = qT[g * 64:(g + 1) * 64]
            parts.append(jnp.where(keep, jnp.concatenate([qg, qg], axis=0), jnp.zeros((), BF16)))
        return jnp.concatenate(parts, axis=1)

    lanes = s_ring.shape[2]
    dv = vT_ref.shape[1]
    ones_rows = jnp.ones((_SUM_ROWS, kb), BF16)

    def scores(qz, i):
        ks = pl.multiple_of(i * kb, kb)
        return jnp.dot(k_ref[0, pl.ds(ks, kb), :], qz, preferred_element_type=F32)

    def weighted_values(i, p):
        ks = pl.multiple_of(i * kb, kb)
        v_ext = jnp.concatenate([vT_ref[0, :, pl.ds(ks, kb)], ones_rows], axis=0)
        return jnp.dot(v_ext, p, preferred_element_type=F32)

    def first_scores(qq):
        qz = padded_queries(qq)
        first = [scores(qz, t) for t in range(_LOOKAHEAD)]
        for t in range(_LOOKAHEAD):
            s_ring[t] = first[t]
        return tuple(jnp.max(f, axis=0, keepdims=True) for f in first)

    def stage(qz, i, slot, stats, with_scores=True):
        alpha_prev, m_prev, maxes = stats
        m_new = jnp.maximum(m_prev, maxes[0])
        if with_scores:
            s_new = scores(qz, i + _LOOKAHEAD)
            s_ring[(slot + _LOOKAHEAD) % _RING] = s_new
            maxes = maxes[1:] + (jnp.max(s_new, axis=0, keepdims=True),)
        else:
            maxes = maxes[1:] + (maxes[0],)
        acc_sc[...] = alpha_prev * acc_sc[...] + weighted_values(jnp.maximum(i - 1, 0), p_ring[(slot + 1) % 2])
        p_ring[slot % 2] = jnp.exp2(s_ring[slot % _RING] - m_new).astype(BF16)
        return jnp.exp2(m_prev - m_new), m_new, maxes

    n_body = nk // unroll

    def query_block(qq, maxes):
        qz = padded_queries(qq)
        p_ring[1] = jnp.zeros(p_ring.shape[1:], BF16)
        acc_sc[...] = jnp.zeros(acc_sc.shape, F32)
        stats = (jnp.ones((1, lanes), F32), jnp.full((1, lanes), NEG, F32), maxes)

        def body(ii, stats):
            for u in range(unroll):
                stats = stage(qz, unroll * ii + u, u, stats)
            return stats

        stats = lax.fori_loop(0, n_body - 1, body, stats)
        for u in range(unroll):
            i_last = unroll * (n_body - 1) + u
            stats = stage(qz, i_last, u, stats, with_scores=i_last + _LOOKAHEAD < nk)
        acc = stats[0] * acc_sc[...] + weighted_values(nk - 1, p_ring[(nk - 1) % 2])

        o = acc[0:dv] / acc[dv:dv + 1]
        rows = pl.ds(pl.multiple_of(qq * qb, qb), qb)
        if variant == "diff":
            lv = lam_ref[...]
            lam = (jnp.exp(jnp.sum(lv[0:1] * lv[1:2], axis=1, keepdims=True))
                   - jnp.exp(jnp.sum(lv[2:3] * lv[3:4], axis=1, keepdims=True)) + lam_init)
            od = o[:, 0:qb] - lam * o[:, qb:2 * qb]
            ms = jnp.mean(od * od, axis=0, keepdims=True)
            od = od * lax.rsqrt(ms + EPS) * gsub_ref[...] * (1.0 - lam_init)
            o_ref[0, rows, :] = od.T.astype(BF16)
        else:
            stacked = jnp.concatenate([o[:, g * qb:(g + 1) * qb] for g in range(4)], axis=0)
            o_ref[0, rows, :] = stacked.T.astype(BF16)
        return first_scores(jnp.minimum(qq + 1, nq - 1))

    lax.fori_loop(0, nq, query_block, first_scores(0))


def _flash(qT, k, vT, variant, lam_vecs=None, gsub_b=None, lam_init=0.0, kb=256, unroll=8):
    b, _, s = qT.shape
    lanes = _FLASH_LANES
    if variant == "diff":
        qb, n_groups, r, dv, width = lanes // 2, 8, 128, 128, 128
        k_map = lambda bi, j: (bi, 0, j)
    else:
        qb, n_groups, r, dv, width = lanes // 4, 4, 256, 64, 256
        k_map = lambda bi, j: (bi, 0, j // 2)
    nk = s // kb
    assert unroll % _RING == 0 and nk % unroll == 0
    kern = functools.partial(_flash_kernel, variant=variant, qb=qb, nq=s // qb, kb=kb, nk=nk,
                             unroll=unroll, lam_init=lam_init)
    in_specs = [
        pl.BlockSpec((1, r, s), lambda bi, j: (bi, j, 0)),
        pl.BlockSpec((1, s, 128), k_map),
        pl.BlockSpec((1, dv, s), lambda bi, j: (bi, j, 0)),
    ]
    args = [qT, k, vT]
    if variant == "diff":
        in_specs += [
            _resident((4, HEAD_DIM), lambda bi, j: (0, 0)),
            _resident((128, qb), lambda bi, j: (0, 0)),
        ]
        args += [lam_vecs, gsub_b]
    return pl.pallas_call(
        kern,
        grid=(b, n_groups),
        in_specs=in_specs,
        out_specs=pl.BlockSpec((1, s, width), lambda bi, j: (bi, 0, j)),
        out_shape=jax.ShapeDtypeStruct((b, s, D_MODEL), BF16),
        scratch_shapes=[
            pltpu.VMEM((_RING, kb, lanes), F32),
            pltpu.VMEM((2, kb, lanes), BF16),
            pltpu.VMEM((dv + _SUM_ROWS, lanes), F32),
        ],
        compiler_params=_cparams(2),
        name="flash_" + variant,
    )(*args)


def _na_table_kernel(rpb_ref, pt_ref):
    cq = lax.broadcasted_iota(jnp.int32, (GRID_W, 128), 0)
    ck = lax.broadcasted_iota(jnp.int32, (GRID_W, 128), 1) & (GRID_W - 1)
    c0 = jnp.clip(cq - NA_WIN_W // 2, 0, GRID_W - NA_WIN_W)
    valid = (ck >= c0) & (ck < c0 + NA_WIN_W)
    for dr in range(2 * NA_WIN_H - 2):
        a = rpb_ref[0, dr:dr + 1, :]
        bb = rpb_ref[0, dr + 1:dr + 2, :]
        x = a + pltpu.roll(bb, GRID_W, axis=1)
        xb = jnp.broadcast_to(x, (GRID_W, 128))
        t = pltpu.roll(xb, 128 - (NA_WIN_W - 1), axis=1, stride=1, stride_axis=0)
        pt_ref[0, dr] = jnp.where(valid, t * LOG2E, NEG)


def _na_table(rpb_pad):
    nh = rpb_pad.shape[0]
    ndr = 2 * NA_WIN_H - 2
    return pl.pallas_call(
        _na_table_kernel,
        grid=(nh,),
        in_specs=[pl.BlockSpec((1, 16, 128), lambda h: (h, 0, 0))],
        out_specs=pl.BlockSpec((1, ndr, GRID_W, 128), lambda h: (h, 0, 0, 0)),
        out_shape=jax.ShapeDtypeStruct((nh, ndr, GRID_W, 128), F32),
        compiler_params=_cparams(1),
        name="na_table",
    )(rpb_pad)


_NA_ROWS = 4
_NA_BAND = _NA_ROWS + NA_WIN_H


def _na_kernel(q_ref, k0_ref, k1_ref, k2_ref, v0_ref, v1_ref, v2_ref, pt_ref, o_ref, kband, vband,
               *, n_steps):
    i = pl.program_id(1)
    blk = _NA_ROWS * GRID_W
    for n, (kr, vr) in enumerate(((k0_ref, v0_ref), (k1_ref, v1_ref), (k2_ref, v2_ref))):
        kband[n * blk:(n + 1) * blk, :] = kr[0]
        vband[n * blk:(n + 1) * blk, :] = vr[0]
    left = lax.broadcasted_iota(jnp.int32, (GRID_W, 128), 1) < HEAD_DIM
    nkeys = NA_WIN_H * GRID_W

    def row_body(t, carry):
        off = jnp.where(i == 0, 0, jnp.where(i == n_steps - 1, _NA_ROWS, t))
        sidx = jnp.where(i == 0, NA_WIN_H - 1 - t, jnp.where(i == n_steps - 1, 3 - t, 3))
        start = pl.multiple_of(off * GRID_W, GRID_W)
        qs = pl.multiple_of(t * GRID_W, GRID_W)
        n_pairs = D_MODEL // 128
        cols = [slice(j * 128, (j + 1) * 128) for j in range(n_pairs)]
        scores = []
        for j in range(n_pairs):
            qp = q_ref[0, pl.ds(qs, GRID_W), cols[j]]
            kp = kband[pl.ds(start, nkeys), cols[j]]
            for e in range(2):
                qz = jnp.where(left if e == 0 else jnp.logical_not(left), qp, jnp.zeros((), BF16))
                scores.append(lax.dot_general(qz, kp, _NT, preferred_element_type=F32))
        probs, denoms = [], []
        for h in range(2 * n_pairs):
            bias = jnp.concatenate([pt_ref[h, sidx + 2 * m] for m in range(NA_WIN_H // 2)], axis=1)
            s = scores[h] + bias
            p = jnp.exp2(s - jnp.max(s, axis=1, keepdims=True))
            denoms.append(jnp.sum(p, axis=1, keepdims=True))
            probs.append(p.astype(BF16))
        for j in range(n_pairs):
            vp = vband[pl.ds(start, nkeys), cols[j]]
            outs = [jnp.dot(probs[2 * j + e], vp, preferred_element_type=F32) / denoms[2 * j + e]
                    for e in range(2)]
            o_ref[0, pl.ds(qs, GRID_W), cols[j]] = jnp.where(left, outs[0], outs[1]).astype(BF16)
        return carry

    lax.fori_loop(0, _NA_ROWS, row_body, 0)


def _na_attention(q, k, v, pt):
    b, s, _ = q.shape
    blk = _NA_ROWS * GRID_W
    n_steps = s // blk
    last = n_steps - 3

    def band_spec(n):
        return pl.BlockSpec((1, blk, D_MODEL),
                            lambda bi, i: (bi, jnp.clip(i - 1, 0, last) + n, 0))

    kern = functools.partial(_na_kernel, n_steps=n_steps)
    return pl.pallas_call(
        kern,
        grid=(b, n_steps),
        in_specs=[
            pl.BlockSpec((1, blk, D_MODEL), lambda bi, i: (bi, i, 0)),
            band_spec(0), band_spec(1), band_spec(2),
            band_spec(0), band_spec(1), band_spec(2),
            _resident(pt.shape, lambda bi, i: (0, 0, 0, 0)),
        ],
        out_specs=pl.BlockSpec((1, blk, D_MODEL), lambda bi, i: (bi, i, 0)),
        out_shape=jax.ShapeDtypeStruct((b, s, D_MODEL), BF16),
        scratch_shapes=[
            pltpu.VMEM((_NA_BAND * GRID_W, D_MODEL), BF16),
            pltpu.VMEM((_NA_BAND * GRID_W, D_MODEL), BF16),
        ],
        compiler_params=_cparams(2),
        name="na_attention",
    )(q, k, k, k, v, v, v, pt)


def _angles(pos, dims, theta):
    inv = theta ** (-jnp.arange(0, dims, 2, dtype=F32) / dims)
    return pos[:, None] * inv[None, :]


def _diff_tables(s):
    ang = _angles(jnp.arange(s, dtype=F32), ROPE_DIMS, ROPE_THETA)
    c, sn = jnp.cos(ang).T, jnp.sin(ang).T
    ones = jnp.ones((HEAD_DIM - ROPE_DIMS, s), F32)
    ct = jnp.concatenate([c, c, ones], axis=0)
    st = jnp.concatenate([-sn, sn, 0.0 * ones], axis=0)
    return ct, st


def _axial_tables(s):
    t = jnp.arange(s)
    half = HEAD_DIM // 2
    ar = _angles((t // GRID_W).astype(F32), half, AXIAL_THETA)
    ac = _angles((t % GRID_W).astype(F32), half, AXIAL_THETA)
    cr, sr, cc, sc = jnp.cos(ar).T, jnp.sin(ar).T, jnp.cos(ac).T, jnp.sin(ac).T
    ct = jnp.concatenate([cr, cr, cc, cc], axis=0)
    st = jnp.concatenate([-sr, sr, -sc, sc], axis=0)
    return ct, st


def _col_gain(g, tm):
    return jnp.broadcast_to(g.astype(F32)[:, None], (g.shape[0], tm))


def kernel(x, norm_g, ffn1_wg, ffn1_wu, ffn1_wd, ffn2_wg, ffn2_wu, ffn2_wd,
           diff_w_in, diff_w_out, diff_lambda, diff_subln,
           na_w_in, na_w_out, na_rpb,
           gqa_w_in, gqa_w_out, gqa_qk_norm):
    b, s, d = x.shape
    n = b * s
    tm = 512
    h = x.reshape(n, d)
    ones_gain = jnp.ones((HEAD_DIM, tm), F32)

    for i in range(DEPTH):
        g = norm_g[i]
        h = _ffn(h, g[0:2], ffn1_wg[i].astype(BF16), ffn1_wu[i].astype(BF16), ffn1_wd[i].astype(BF16))
        kind, j = i % N_MIXERS, i // N_MIXERS
        h3 = h.reshape(b, s, d)
        if kind == 0:
            w = diff_w_in[j]
            wqT = w[:, 0:d].T.astype(BF16)
            wkT = w[:, d:2 * d].T.astype(BF16)
            wvT = w[:, 2 * d:3 * d].T.astype(BF16)
            ct, st = _diff_tables(s)
            qT, k, vT = _inproj_t(h3, g[2:3], wqT, wkT, wvT, ct, st, ones_gain, ones_gain, "diff", tm)
            lam_init = 0.8 - 0.6 * math.exp(-0.3 * i)
            gsub_b = _col_gain(diff_subln[j], _FLASH_LANES // 2)
            o = _flash(qT, k, vT, "diff", diff_lambda[j].astype(F32), gsub_b, lam_init)
            w_out = diff_w_out[j]
        elif kind == 1:
            q, k, v = _inproj_n(h3, g[2:3], na_w_in[j].astype(BF16), tm)
            rpb = na_rpb[j].astype(F32)
            rpb_pad = jnp.zeros((rpb.shape[0], 16, 128), F32).at[:, :rpb.shape[1], :rpb.shape[2]].set(rpb)
            pt = _na_table(rpb_pad)
            o = _na_attention(q, k, v, pt)
            w_out = na_w_out[j]
        else:
            w = gqa_w_in[j]
            nq = d
            nkv = (w.shape[1] - nq) // 2
            wqT = w[:, 0:nq].T.astype(BF16)
            wkT = w[:, nq:nq + nkv].T.astype(BF16)
            wvT = w[:, nq + nkv:].T.astype(BF16)
            ct, st = _axial_tables(s)
            gq_b = _col_gain(gqa_qk_norm[j, 0], tm)
            gk_b = _col_gain(gqa_qk_norm[j, 1], tm)
            qT, k, vT = _inproj_t(h3, g[2:3], wqT, wkT, wvT, ct, st, gq_b, gk_b, "gqa", tm)
            o = _flash(qT, k, vT, "gqa")
            w_out = gqa_w_out[j]
        h = _outproj_ffn(o.reshape(n, d), h, g[3:6], w_out.astype(BF16),
                         ffn2_wg[i].astype(BF16), ffn2_wu[i].astype(BF16), ffn2_wd[i].astype(BF16), tm)
    return h.reshape(b, s, d)
```

```python
import functools
import math

import jax
import jax.numpy as jnp
from jax import lax
from jax.experimental import pallas as pl
from jax.experimental.pallas import tpu as pltpu

D_MODEL = 1024
SEQ = 8192
DEPTH = 4
N_MIXERS = 3
GRID_W = 64
EPS = 1e-6
D_FF = 2816
FFN_RES = 0.5
HEAD_DIM = 64
ROPE_THETA = 500000.0
ROPE_DIMS = 16
NA_WIN_H = 8
NA_WIN_W = 16
AXIAL_THETA = 10000.0

LOG2E = 1.4426950408889634
QSCALE = (HEAD_DIM ** -0.5) * LOG2E
NEG = -1e30

VMEM_LIMIT_BYTES = 56 * 1024 * 1024

BF16 = jnp.bfloat16
F32 = jnp.float32

_NT = (((1,), (1,)), ((), ()))


def _cparams(n_axes):
    return pltpu.CompilerParams(
        dimension_semantics=("parallel",) * n_axes,
        vmem_limit_bytes=VMEM_LIMIT_BYTES,
    )


def _resident(shape, index_map):
    return pl.BlockSpec(shape, index_map, pipeline_mode=pl.Buffered(1))


def _rms_rows(x, g):
    ms = jnp.mean(x * x, axis=-1, keepdims=True)
    return x * lax.rsqrt(ms + EPS) * g


def _ffn_block(x, g_pre, g_post, wg_ref, wu_ref, wd_ref):
    xn = _rms_rows(x, g_pre).astype(BF16)
    gate = jnp.dot(xn, wg_ref[...], preferred_element_type=F32)
    up = jnp.dot(xn, wu_ref[...], preferred_element_type=F32)
    act = (gate * jax.nn.sigmoid(gate) * up).astype(BF16)
    y = jnp.dot(act, wd_ref[...], preferred_element_type=F32)
    return x + FFN_RES * _rms_rows(y, g_post)


def _ffn_kernel(x_ref, g_ref, wg_ref, wu_ref, wd_ref, o_ref):
    g = g_ref[...]
    o_ref[...] = _ffn_block(x_ref[...], g[0:1], g[1:2], wg_ref, wu_ref, wd_ref)


def _outproj_ffn_kernel(o_ref, h_ref, g_ref, wo_ref, wg_ref, wu_ref, wd_ref, out_ref):
    g = g_ref[...]
    m = jnp.dot(o_ref[...], wo_ref[...], preferred_element_type=F32)
    h1 = h_ref[...] + _rms_rows(m, g[0:1])
    out_ref[...] = _ffn_block(h1, g[1:2], g[2:3], wg_ref, wu_ref, wd_ref)


def _outproj_ffn(o2d, h2d, g3, wo, wg, wu, wd, tm=512):
    n = h2d.shape[0]
    blk = pl.BlockSpec((tm, D_MODEL), lambda i: (i, 0))
    return pl.pallas_call(
        _outproj_ffn_kernel,
        grid=(n // tm,),
        in_specs=[
            blk, blk,
            _resident((3, D_MODEL), lambda i: (0, 0)),
            _resident((D_MODEL, D_MODEL), lambda i: (0, 0)),
            _resident((D_MODEL, D_FF), lambda i: (0, 0)),
            _resident((D_MODEL, D_FF), lambda i: (0, 0)),
            _resident((D_FF, D_MODEL), lambda i: (0, 0)),
        ],
        out_specs=blk,
        out_shape=jax.ShapeDtypeStruct(h2d.shape, F32),
        compiler_params=_cparams(1),
        name="outproj_ffn",
    )(o2d, h2d, g3, wo, wg, wu, wd)


def _ffn(h2d, g2, wg, wu, wd, tm=512):
    n = h2d.shape[0]
    return pl.pallas_call(
        _ffn_kernel,
        grid=(n // tm,),
        in_specs=[
            pl.BlockSpec((tm, D_MODEL), lambda i: (i, 0)),
            _resident((2, D_MODEL), lambda i: (0, 0)),
            _resident((D_MODEL, D_FF), lambda i: (0, 0)),
            _resident((D_MODEL, D_FF), lambda i: (0, 0)),
            _resident((D_FF, D_MODEL), lambda i: (0, 0)),
        ],
        out_specs=pl.BlockSpec((tm, D_MODEL), lambda i: (i, 0)),
        out_shape=jax.ShapeDtypeStruct(h2d.shape, F32),
        compiler_params=_cparams(1),
        name="ffn",
    )(h2d, g2, wg, wu, wd)


def _swap_halves(x, half):
    blocks = []
    r = 0
    n = x.shape[0]
    while r + 2 * half <= n:
        blocks += [x[r + half:r + 2 * half], x[r:r + half]]
        r += 2 * half
    return jnp.concatenate(blocks, axis=0)


def _inproj_t_kernel(h_ref, g_ref, wqT_ref, wkT_ref, wvT_ref, ct_ref, st_ref, gq_ref, gk_ref,
                     qT_ref, k_ref, vT_ref, *, variant):
    u = _rms_rows(h_ref[0], g_ref[...]).astype(BF16)
    ct = ct_ref[...]
    st = st_ref[...]

    def head(x, gain, scale):
        if variant == "gqa":
            ms = jnp.mean(x * x, axis=0, keepdims=True)
            x = x * lax.rsqrt(ms + EPS) * gain
            xs = _swap_halves(x, 16)
        else:
            xs = jnp.concatenate([x[8:16], x[0:8], x[16:64]], axis=0)
        x = x * ct + xs * st
        if scale is not None:
            x = x * scale
        return x

    gq = gq_ref[...]
    gk = gk_ref[...]

    qT = lax.dot_general(wqT_ref[...], u, _NT, preferred_element_type=F32)
    for hh in range(qT.shape[0] // HEAD_DIM):
        sl = slice(hh * HEAD_DIM, (hh + 1) * HEAD_DIM)
        qT_ref[0, sl, :] = head(qT[sl], gq, QSCALE).astype(BF16)

    kT = lax.dot_general(wkT_ref[...], u, _NT, preferred_element_type=F32)
    for pp in range(kT.shape[0] // 128):
        pair = jnp.concatenate(
            [head(kT[pp * 128 + e * 64: pp * 128 + (e + 1) * 64], gk, None) for e in range(2)], axis=0)
        k_ref[0, :, pp * 128:(pp + 1) * 128] = pair.T.astype(BF16)

    vT = lax.dot_general(wvT_ref[...], u, _NT, preferred_element_type=F32)
    vT_ref[0] = vT.astype(BF16)


def _inproj_t(h3d, g, wqT, wkT, wvT, ct, st, gq_b, gk_b, variant, tm=512):
    b, s, _ = h3d.shape
    nq, nk, nv = wqT.shape[0], wkT.shape[0], wvT.shape[0]
    kern = functools.partial(_inproj_t_kernel, variant=variant)
    return pl.pallas_call(
        kern,
        grid=(b, s // tm),
        in_specs=[
            pl.BlockSpec((1, tm, D_MODEL), lambda bi, ti: (bi, ti, 0)),
            _resident((1, D_MODEL), lambda bi, ti: (0, 0)),
            _resident((nq, D_MODEL), lambda bi, ti: (0, 0)),
            _resident((nk, D_MODEL), lambda bi, ti: (0, 0)),
            _resident((nv, D_MODEL), lambda bi, ti: (0, 0)),
            pl.BlockSpec((HEAD_DIM, tm), lambda bi, ti: (0, ti)),
            pl.BlockSpec((HEAD_DIM, tm), lambda bi, ti: (0, ti)),
            _resident((HEAD_DIM, tm), lambda bi, ti: (0, 0)),
            _resident((HEAD_DIM, tm), lambda bi, ti: (0, 0)),
        ],
        out_specs=[
            pl.BlockSpec((1, nq, tm), lambda bi, ti: (bi, 0, ti)),
            pl.BlockSpec((1, tm, nk), lambda bi, ti: (bi, ti, 0)),
            pl.BlockSpec((1, nv, tm), lambda bi, ti: (bi, 0, ti)),
        ],
        out_shape=[
            jax.ShapeDtypeStruct((b, nq, s), BF16),
            jax.ShapeDtypeStruct((b, s, nk), BF16),
            jax.ShapeDtypeStruct((b, nv, s), BF16),
        ],
        compiler_params=_cparams(2),
        name="inproj_" + variant,
    )(h3d, g, wqT, wkT, wvT, ct, st, gq_b, gk_b)


def _inproj_n_kernel(h_ref, g_ref, w_ref, q_ref, k_ref, v_ref):
    u = _rms_rows(h_ref[0], g_ref[...]).astype(BF16)
    y = jnp.dot(u, w_ref[...], preferred_element_type=F32)
    q_ref[0] = (y[:, 0:D_MODEL] * QSCALE).astype(BF16)
    k_ref[0] = y[:, D_MODEL:2 * D_MODEL].astype(BF16)
    v_ref[0] = y[:, 2 * D_MODEL:3 * D_MODEL].astype(BF16)


def _inproj_n(h3d, g, w, tm=512):
    b, s, _ = h3d.shape
    blk = pl.BlockSpec((1, tm, D_MODEL), lambda bi, ti: (bi, ti, 0))
    return pl.pallas_call(
        _inproj_n_kernel,
        grid=(b, s // tm),
        in_specs=[
            blk,
            _resident((1, D_MODEL), lambda bi, ti: (0, 0)),
            _resident((D_MODEL, 3 * D_MODEL), lambda bi, ti: (0, 0)),
        ],
        out_specs=[blk, blk, blk],
        out_shape=[jax.ShapeDtypeStruct((b, s, D_MODEL), BF16)] * 3,
        compiler_params=_cparams(2),
        name="inproj_na",
    )(h3d, g, w)


_SUM_ROWS = 16
_RING = 4
_LOOKAHEAD = 3
_PV_LAG = 1
_P_RING = 2
_FLASH_LANES = 2048


def _flash_kernel(qT_ref, k_ref, vT_ref, *rest, variant, qb, nq, kb, nk, unroll, lam_init):
    if variant == "diff":
        lam_ref, gsub_ref, o_ref, s_ring, p_ring, acc_sc = rest
    else:
        o_ref, s_ring, p_ring, acc_sc = rest

    def padded_queries(qq):
        qT = qT_ref[0, :, pl.ds(pl.multiple_of(qq * qb, qb), qb)]
        if variant == "diff":
            zero = jnp.zeros((HEAD_DIM, qb), BF16)
            return jnp.concatenate(
                [jnp.concatenate([qT[0:64], zero], axis=0),
                 jnp.concatenate([zero, qT[64:128]], axis=0)], axis=1)
        parity = pl.program_id(1) % 2
        row_half = (lax.broadcasted_iota(jnp.int32, (128, qb), 0) >= HEAD_DIM).astype(jnp.int32)
        keep = row_half == parity
        parts = []
        for g in range(4):
            qg = qT[g * 64:(g + 1) * 64]
            parts.append(jnp.where(keep, jnp.concatenate([qg, qg], axis=0), jnp.zeros((), BF16)))
        return jnp.concatenate(parts, axis=1)

    lanes = s_ring.shape[2]
    dv = vT_ref.shape[1]
    ones_rows = jnp.ones((_SUM_ROWS, kb), BF16)

    def scores(qz, i):
        ks = pl.multiple_of(i * kb, kb)
        return jnp.dot(k_ref[0, pl.ds(ks, kb), :], qz, preferred_element_type=F32)

    def weighted_values(i, p):
        ks = pl.multiple_of(i * kb, kb)
        v_ext = jnp.concatenate([vT_ref[0, :, pl.ds(ks, kb)], ones_rows], axis=0)
        return jnp.dot(v_ext, p, preferred_element_type=F32)

    def first_scores(qq):
        qz = padded_queries(qq)
        first = [scores(qz, t) for t in range(_LOOKAHEAD)]
        for t in range(_LOOKAHEAD):
            s_ring[t] = first[t]
        return tuple(jnp.max(f, axis=0, keepdims=True) for f in first)

    def stage(qz, i, slot, stats, with_scores=True):
        alphas, m_prev, maxes = stats
        m_new = jnp.maximum(m_prev, maxes[0])
        if with_scores:
            s_new = scores(qz, i + _LOOKAHEAD)
            s_ring[(slot + _LOOKAHEAD) % _RING] = s_new
            maxes = maxes[1:] + (jnp.max(s_new, axis=0, keepdims=True),)
        else:
            maxes = maxes[1:] + (maxes[0],)
        acc_sc[...] = alphas[0] * acc_sc[...] + weighted_values(
            jnp.maximum(i - _PV_LAG, 0), p_ring[(slot - _PV_LAG) % _P_RING])
        p_ring[slot % _P_RING] = jnp.exp2(s_ring[slot % _RING] - m_new).astype(BF16)
        return alphas[1:] + (jnp.exp2(m_prev - m_new),), m_new, maxes

    n_body = nk // unroll

    def query_block(qq, maxes):
        qz = padded_queries(qq)
        for t in range(1, _PV_LAG + 1):
            p_ring[-t % _P_RING] = jnp.zeros(p_ring.shape[1:], BF16)
        acc_sc[...] = jnp.zeros(acc_sc.shape, F32)
        stats = ((jnp.ones((1, lanes), F32),) * _PV_LAG, jnp.full((1, lanes), NEG, F32), maxes)

        def body(ii, stats):
            for u in range(unroll):
                stats = stage(qz, unroll * ii + u, u, stats)
            return stats

        stats = lax.fori_loop(0, n_body - 1, body, stats)
        for u in range(unroll):
            i_last = unroll * (n_body - 1) + u
            stats = stage(qz, i_last, u, stats, with_scores=i_last + _LOOKAHEAD < nk)
        acc = acc_sc[...]
        for t in range(_PV_LAG):
            blk = nk - _PV_LAG + t
            acc = stats[0][t] * acc + weighted_values(blk, p_ring[blk % _P_RING])

        o = acc[0:dv] / acc[dv:dv + 1]
        rows = pl.ds(pl.multiple_of(qq * qb, qb), qb)
        if variant == "diff":
            lv = lam_ref[...]
            lam = (jnp.exp(jnp.sum(lv[0:1] * lv[1:2], axis=1, keepdims=True))
                   - jnp.exp(jnp.sum(lv[2:3] * lv[3:4], axis=1, keepdims=True)) + lam_init)
            od = o[:, 0:qb] - lam * o[:, qb:2 * qb]
            ms = jnp.mean(od * od, axis=0, keepdims=True)
            od = od * lax.rsqrt(ms + EPS) * gsub_ref[...] * (1.0 - lam_init)
            o_ref[0, rows, :] = od.T.astype(BF16)
        else:
            stacked = jnp.concatenate([o[:, g * qb:(g + 1) * qb] for g in range(4)], axis=0)
            o_ref[0, rows, :] = stacked.T.astype(BF16)
        return first_scores(jnp.minimum(qq + 1, nq - 1))

    lax.fori_loop(0, nq, query_block, first_scores(0))


def _flash(qT, k, vT, variant, lam_vecs=None, gsub_b=None, lam_init=0.0, kb=256, unroll=8):
    b, _, s = qT.shape
    lanes = _FLASH_LANES
    if variant == "diff":
        qb, n_groups, r, dv, width = lanes // 2, 8, 128, 128, 128
        k_map = lambda bi, j: (bi, 0, j)
    else:
        qb, n_groups, r, dv, width = lanes // 4, 4, 256, 64, 256
        k_map = lambda bi, j: (bi, 0, j // 2)
    nk = s // kb
    assert unroll % _RING == 0 and nk % unroll == 0
    kern = functools.partial(_flash_kernel, variant=variant, qb=qb, nq=s // qb, kb=kb, nk=nk,
                             unroll=unroll, lam_init=lam_init)
    in_specs = [
        pl.BlockSpec((1, r, s), lambda bi, j: (bi, j, 0)),
        pl.BlockSpec((1, s, 128), k_map),
        pl.BlockSpec((1, dv, s), lambda bi, j: (bi, j, 0)),
    ]
    args = [qT, k, vT]
    if variant == "diff":
        in_specs += [
            _resident((4, HEAD_DIM), lambda bi, j: (0, 0)),
            _resident((128, qb), lambda bi, j: (0, 0)),
        ]
        args += [lam_vecs, gsub_b]
    return pl.pallas_call(
        kern,
        grid=(b, n_groups),
        in_specs=in_specs,
        out_specs=pl.BlockSpec((1, s, width), lambda bi, j: (bi, 0, j)),
        out_shape=jax.ShapeDtypeStruct((b, s, D_MODEL), BF16),
        scratch_shapes=[
            pltpu.VMEM((_RING, kb, lanes), F32),
            pltpu.VMEM((_P_RING, kb, lanes), BF16),
            pltpu.VMEM((dv + _SUM_ROWS, lanes), F32),
        ],
        compiler_params=_cparams(2),
        name="flash_" + variant,
    )(*args)


def _na_table_kernel(rpb_ref, pt_ref):
    cq = lax.broadcasted_iota(jnp.int32, (GRID_W, 128), 0)
    ck = lax.broadcasted_iota(jnp.int32, (GRID_W, 128), 1) & (GRID_W - 1)
    c0 = jnp.clip(cq - NA_WIN_W // 2, 0, GRID_W - NA_WIN_W)
    valid = (ck >= c0) & (ck < c0 + NA_WIN_W)
    for dr in range(2 * NA_WIN_H - 2):
        a = rpb_ref[0, dr:dr + 1, :]
        bb = rpb_ref[0, dr + 1:dr + 2, :]
        x = a + pltpu.roll(bb, GRID_W, axis=1)
        xb = jnp.broadcast_to(x, (GRID_W, 128))
        t = pltpu.roll(xb, 128 - (NA_WIN_W - 1), axis=1, stride=1, stride_axis=0)
        pt_ref[0, dr] = jnp.where(valid, t * LOG2E, NEG)


def _na_table(rpb_pad):
    nh = rpb_pad.shape[0]
    ndr = 2 * NA_WIN_H - 2
    return pl.pallas_call(
        _na_table_kernel,
        grid=(nh,),
        in_specs=[pl.BlockSpec((1, 16, 128), lambda h: (h, 0, 0))],
        out_specs=pl.BlockSpec((1, ndr, GRID_W, 128), lambda h: (h, 0, 0, 0)),
        out_shape=jax.ShapeDtypeStruct((nh, ndr, GRID_W, 128), F32),
        compiler_params=_cparams(1),
        name="na_table",
    )(rpb_pad)


_NA_ROWS = 4
_NA_BAND = _NA_ROWS + NA_WIN_H


def _na_kernel(q_ref, k0_ref, k1_ref, k2_ref, v0_ref, v1_ref, v2_ref, pt_ref, o_ref, kband, vband,
               *, n_steps):
    i = pl.program_id(1)
    blk = _NA_ROWS * GRID_W
    for n, (kr, vr) in enumerate(((k0_ref, v0_ref), (k1_ref, v1_ref), (k2_ref, v2_ref))):
        kband[n * blk:(n + 1) * blk, :] = kr[0]
        vband[n * blk:(n + 1) * blk, :] = vr[0]
    left = lax.broadcasted_iota(jnp.int32, (GRID_W, 128), 1) < HEAD_DIM
    nkeys = NA_WIN_H * GRID_W

    def row_body(t, carry):
        off = jnp.where(i == 0, 0, jnp.where(i == n_steps - 1, _NA_ROWS, t))
        sidx = jnp.where(i == 0, NA_WIN_H - 1 - t, jnp.where(i == n_steps - 1, 3 - t, 3))
        start = pl.multiple_of(off * GRID_W, GRID_W)
        qs = pl.multiple_of(t * GRID_W, GRID_W)
        n_pairs = D_MODEL // 128
        cols = [slice(j * 128, (j + 1) * 128) for j in range(n_pairs)]
        scores = []
        for j in range(n_pairs):
            qp = q_ref[0, pl.ds(qs, GRID_W), cols[j]]
            kp = kband[pl.ds(start, nkeys), cols[j]]
            for e in range(2):
                qz = jnp.where(left if e == 0 else jnp.logical_not(left), qp, jnp.zeros((), BF16))
                scores.append(lax.dot_general(qz, kp, _NT, preferred_element_type=F32))
        probs, denoms = [], []
        for h in range(2 * n_pairs):
            bias = jnp.concatenate([pt_ref[h, sidx + 2 * m] for m in range(NA_WIN_H // 2)], axis=1)
            s = scores[h] + bias
            p = jnp.exp2(s - jnp.max(s, axis=1, keepdims=True))
            denoms.append(jnp.sum(p, axis=1, keepdims=True))
            probs.append(p.astype(BF16))
        for j in range(n_pairs):
            vp = vband[pl.ds(start, nkeys), cols[j]]
            outs = [jnp.dot(probs[2 * j + e], vp, preferred_element_type=F32) / denoms[2 * j + e]
                    for e in range(2)]
            o_ref[0, pl.ds(qs, GRID_W), cols[j]] = jnp.where(left, outs[0], outs[1]).astype(BF16)
        return carry

    lax.fori_loop(0, _NA_ROWS, row_body, 0)


def _na_attention(q, k, v, pt):
    b, s, _ = q.shape
    blk = _NA_ROWS * GRID_W
    n_steps = s // blk
    last = n_steps - 3

    def band_spec(n):
        return pl.BlockSpec((1, blk, D_MODEL),
                            lambda bi, i: (bi, jnp.clip(i - 1, 0, last) + n, 0))

    kern = functools.partial(_na_kernel, n_steps=n_steps)
    return pl.pallas_call(
        kern,
        grid=(b, n_steps),
        in_specs=[
            pl.BlockSpec((1, blk, D_MODEL), lambda bi, i: (bi, i, 0)),
            band_spec(0), band_spec(1), band_spec(2),
            band_spec(0), band_spec(1), band_spec(2),
            _resident(pt.shape, lambda bi, i: (0, 0, 0, 0)),
        ],
        out_specs=pl.BlockSpec((1, blk, D_MODEL), lambda bi, i: (bi, i, 0)),
        out_shape=jax.ShapeDtypeStruct((b, s, D_MODEL), BF16),
        scratch_shapes=[
            pltpu.VMEM((_NA_BAND * GRID_W, D_MODEL), BF16),
            pltpu.VMEM((_NA_BAND * GRID_W, D_MODEL), BF16),
        ],
        compiler_params=_cparams(2),
        name="na_attention",
    )(q, k, k, k, v, v, v, pt)


def _angles(pos, dims, theta):
    inv = theta ** (-jnp.arange(0, dims, 2, dtype=F32) / dims)
    return pos[:, None] * inv[None, :]


def _diff_tables(s):
    ang = _angles(jnp.arange(s, dtype=F32), ROPE_DIMS, ROPE_THETA)
    c, sn = jnp.cos(ang).T, jnp.sin(ang).T
    ones = jnp.ones((HEAD_DIM - ROPE_DIMS, s), F32)
    ct = jnp.concatenate([c, c, ones], axis=0)
    st = jnp.concatenate([-sn, sn, 0.0 * ones], axis=0)
    return ct, st


def _axial_tables(s):
    t = jnp.arange(s)
    half = HEAD_DIM // 2
    ar = _angles((t // GRID_W).astype(F32), half, AXIAL_THETA)
    ac = _angles((t % GRID_W).astype(F32), half, AXIAL_THETA)
    cr, sr, cc, sc = jnp.cos(ar).T, jnp.sin(ar).T, jnp.cos(ac).T, jnp.sin(ac).T
    ct = jnp.concatenate([cr, cr, cc, cc], axis=0)
    st = jnp.concatenate([-sr, sr, -sc, sc], axis=0)
    return ct, st


def _col_gain(g, tm):
    return jnp.broadcast_to(g.astype(F32)[:, None], (g.shape[0], tm))


def kernel(x, norm_g, ffn1_wg, ffn1_wu, ffn1_wd, ffn2_wg, ffn2_wu, ffn2_wd,
           diff_w_in, diff_w_out, diff_lambda, diff_subln,
           na_w_in, na_w_out, na_rpb,
           gqa_w_in, gqa_w_out, gqa_qk_norm):
    b, s, d = x.shape
    n = b * s
    tm = 512
    h = x.reshape(n, d)
    ones_gain = jnp.ones((HEAD_DIM, tm), F32)

    for i in range(DEPTH):
        g = norm_g[i]
        h = _ffn(h, g[0:2], ffn1_wg[i].astype(BF16), ffn1_wu[i].astype(BF16), ffn1_wd[i].astype(BF16))
        kind, j = i % N_MIXERS, i // N_MIXERS
        h3 = h.reshape(b, s, d)
        if kind == 0:
            w = diff_w_in[j]
            wqT = w[:, 0:d].T.astype(BF16)
            wkT = w[:, d:2 * d].T.astype(BF16)
            wvT = w[:, 2 * d:3 * d].T.astype(BF16)
            ct, st = _diff_tables(s)
            qT, k, vT = _inproj_t(h3, g[2:3], wqT, wkT, wvT, ct, st, ones_gain, ones_gain, "diff", tm)
            lam_init = 0.8 - 0.6 * math.exp(-0.3 * i)
            gsub_b = _col_gain(diff_subln[j], _FLASH_LANES // 2)
            o = _flash(qT, k, vT, "diff", diff_lambda[j].astype(F32), gsub_b, lam_init)
            w_out = diff_w_out[j]
        elif kind == 1:
            q, k, v = _inproj_n(h3, g[2:3], na_w_in[j].astype(BF16), tm)
            rpb = na_rpb[j].astype(F32)
            rpb_pad = jnp.zeros((rpb.shape[0], 16, 128), F32).at[:, :rpb.shape[1], :rpb.shape[2]].set(rpb)
            pt = _na_table(rpb_pad)
            o = _na_attention(q, k, v, pt)
            w_out = na_w_out[j]
        else:
            w = gqa_w_in[j]
            nq = d
            nkv = (w.shape[1] - nq) // 2
            wqT = w[:, 0:nq].T.astype(BF16)
            wkT = w[:, nq:nq + nkv].T.astype(BF16)
            wvT = w[:, nq + nkv:].T.astype(BF16)
            ct, st = _axial_tables(s)
            gq_b = _col_gain(gqa_qk_norm[j, 0], tm)
            gk_b = _col_gain(gqa_qk_norm[j, 1], tm)
            qT, k, vT = _inproj_t(h3, g[2:3], wqT, wkT, wvT, ct, st, gq_b, gk_b, "gqa", tm)
            o = _flash(qT, k, vT, "gqa")
            w_out = gqa_w_out[j]
        h = _outproj_ffn(o.reshape(n, d), h, g[3:6], w_out.astype(BF16),
                         ffn2_wg[i].astype(BF16), ffn2_wu[i].astype(BF16), ffn2_wd[i].astype(BF16), tm)
    return h.reshape(b, s, d)
```

```python
import functools
import math

import jax
import jax.numpy as jnp
from jax import lax
from jax.experimental import pallas as pl
from jax.experimental.pallas import tpu as pltpu

D_MODEL = 1024
SEQ = 8192
DEPTH = 4
N_MIXERS = 3
GRID_W = 64
EPS = 1e-6
D_FF = 2816
FFN_RES = 0.5
HEAD_DIM = 64
ROPE_THETA = 500000.0
ROPE_DIMS = 16
NA_WIN_H = 8
NA_WIN_W = 16
AXIAL_THETA = 10000.0

LOG2E = 1.4426950408889634
QSCALE = (HEAD_DIM ** -0.5) * LOG2E
NEG = -1e30

VMEM_LIMIT_BYTES = 56 * 1024 * 1024

BF16 = jnp.bfloat16
F32 = jnp.float32

_NT = (((1,), (1,)), ((), ()))


def _cparams(n_axes):
    return pltpu.CompilerParams(
        dimension_semantics=("parallel",) * n_axes,
        vmem_limit_bytes=VMEM_LIMIT_BYTES,
    )


def _resident(shape, index_map):
    return pl.BlockSpec(shape, index_map, pipeline_mode=pl.Buffered(1))


def _rms_rows(x, g):
    ms = jnp.mean(x * x, axis=-1, keepdims=True)
    return x * lax.rsqrt(ms + EPS) * g


def _ffn_block(x, g_pre, g_post, wg_ref, wu_ref, wd_ref):
    xn = _rms_rows(x, g_pre).astype(BF16)
    gate = jnp.dot(xn, wg_ref[...], preferred_element_type=F32)
    up = jnp.dot(xn, wu_ref[...], preferred_element_type=F32)
    act = (gate * jax.nn.sigmoid(gate) * up).astype(BF16)
    y = jnp.dot(act, wd_ref[...], preferred_element_type=F32)
    return x + FFN_RES * _rms_rows(y, g_post)


def _ffn_kernel(x_ref, g_ref, wg_ref, wu_ref, wd_ref, o_ref):
    g = g_ref[...]
    o_ref[...] = _ffn_block(x_ref[...], g[0:1], g[1:2], wg_ref, wu_ref, wd_ref)


def _outproj_ffn_kernel(o_ref, h_ref, g_ref, wo_ref, wg_ref, wu_ref, wd_ref, out_ref):
    g = g_ref[...]
    m = jnp.dot(o_ref[...], wo_ref[...], preferred_element_type=F32)
    h1 = h_ref[...] + _rms_rows(m, g[0:1])
    out_ref[...] = _ffn_block(h1, g[1:2], g[2:3], wg_ref, wu_ref, wd_ref)


def _outproj_ffn(o2d, h2d, g3, wo, wg, wu, wd, tm=512):
    n = h2d.shape[0]
    blk = pl.BlockSpec((tm, D_MODEL), lambda i: (i, 0))
    return pl.pallas_call(
        _outproj_ffn_kernel,
        grid=(n // tm,),
        in_specs=[
            blk, blk,
            _resident((3, D_MODEL), lambda i: (0, 0)),
            _resident((D_MODEL, D_MODEL), lambda i: (0, 0)),
            _resident((D_MODEL, D_FF), lambda i: (0, 0)),
            _resident((D_MODEL, D_FF), lambda i: (0, 0)),
            _resident((D_FF, D_MODEL), lambda i: (0, 0)),
        ],
        out_specs=blk,
        out_shape=jax.ShapeDtypeStruct(h2d.shape, F32),
        compiler_params=_cparams(1),
        name="outproj_ffn",
    )(o2d, h2d, g3, wo, wg, wu, wd)


def _ffn(h2d, g2, wg, wu, wd, tm=512):
    n = h2d.shape[0]
    return pl.pallas_call(
        _ffn_kernel,
        grid=(n // tm,),
        in_specs=[
            pl.BlockSpec((tm, D_MODEL), lambda i: (i, 0)),
            _resident((2, D_MODEL), lambda i: (0, 0)),
            _resident((D_MODEL, D_FF), lambda i: (0, 0)),
            _resident((D_MODEL, D_FF), lambda i: (0, 0)),
            _resident((D_FF, D_MODEL), lambda i: (0, 0)),
        ],
        out_specs=pl.BlockSpec((tm, D_MODEL), lambda i: (i, 0)),
        out_shape=jax.ShapeDtypeStruct(h2d.shape, F32),
        compiler_params=_cparams(1),
        name="ffn",
    )(h2d, g2, wg, wu, wd)


def _swap_halves(x, half):
    blocks = []
    r = 0
    n = x.shape[0]
    while r + 2 * half <= n:
        blocks += [x[r + half:r + 2 * half], x[r:r + half]]
        r += 2 * half
    return jnp.concatenate(blocks, axis=0)


def _inproj_t_kernel(h_ref, g_ref, wqT_ref, wkT_ref, wvT_ref, ct_ref, st_ref, gq_ref, gk_ref,
                     qT_ref, k_ref, vT_ref, *, variant):
    u = _rms_rows(h_ref[0], g_ref[...]).astype(BF16)
    ct = ct_ref[...]
    st = st_ref[...]

    def head(x, gain, scale):
        if variant == "gqa":
            ms = jnp.mean(x * x, axis=0, keepdims=True)
            x = x * lax.rsqrt(ms + EPS) * gain
            xs = _swap_halves(x, 16)
        else:
            xs = jnp.concatenate([x[8:16], x[0:8], x[16:64]], axis=0)
        x = x * ct + xs * st
        if scale is not None:
            x = x * scale
        return x

    gq = gq_ref[...]
    gk = gk_ref[...]

    qT = lax.dot_general(wqT_ref[...], u, _NT, preferred_element_type=F32)
    for hh in range(qT.shape[0] // HEAD_DIM):
        sl = slice(hh * HEAD_DIM, (hh + 1) * HEAD_DIM)
        qT_ref[0, sl, :] = head(qT[sl], gq, QSCALE).astype(BF16)

    kT = lax.dot_general(wkT_ref[...], u, _NT, preferred_element_type=F32)
    for pp in range(kT.shape[0] // 128):
        pair = jnp.concatenate(
            [head(kT[pp * 128 + e * 64: pp * 128 + (e + 1) * 64], gk, None) for e in range(2)], axis=0)
        k_ref[0, :, pp * 128:(pp + 1) * 128] = pair.T.astype(BF16)

    vT = lax.dot_general(wvT_ref[...], u, _NT, preferred_element_type=F32)
    vT_ref[0] = vT.astype(BF16)


def _inproj_t(h3d, g, wqT, wkT, wvT, ct, st, gq_b, gk_b, variant, tm=512):
    b, s, _ = h3d.shape
    nq, nk, nv = wqT.shape[0], wkT.shape[0], wvT.shape[0]
    kern = functools.partial(_inproj_t_kernel, variant=variant)
    return pl.pallas_call(
        kern,
        grid=(b, s // tm),
        in_specs=[
            pl.BlockSpec((1, tm, D_MODEL), lambda bi, ti: (bi, ti, 0)),
            _resident((1, D_MODEL), lambda bi, ti: (0, 0)),
            _resident((nq, D_MODEL), lambda bi, ti: (0, 0)),
            _resident((nk, D_MODEL), lambda bi, ti: (0, 0)),
            _resident((nv, D_MODEL), lambda bi, ti: (0, 0)),
            pl.BlockSpec((HEAD_DIM, tm), lambda bi, ti: (0, ti)),
            pl.BlockSpec((HEAD_DIM, tm), lambda bi, ti: (0, ti)),
            _resident((HEAD_DIM, tm), lambda bi, ti: (0, 0)),
            _resident((HEAD_DIM, tm), lambda bi, ti: (0, 0)),
        ],
        out_specs=[
            pl.BlockSpec((1, nq, tm), lambda bi, ti: (bi, 0, ti)),
            pl.BlockSpec((1, tm, nk), lambda bi, ti: (bi, ti, 0)),
            pl.BlockSpec((1, nv, tm), lambda bi, ti: (bi, 0, ti)),
        ],
        out_shape=[
            jax.ShapeDtypeStruct((b, nq, s), BF16),
            jax.ShapeDtypeStruct((b, s, nk), BF16),
            jax.ShapeDtypeStruct((b, nv, s), BF16),
        ],
        compiler_params=_cparams(2),
        name="inproj_" + variant,
    )(h3d, g, wqT, wkT, wvT, ct, st, gq_b, gk_b)


def _inproj_n_kernel(h_ref, g_ref, w_ref, q_ref, k_ref, v_ref):
    u = _rms_rows(h_ref[0], g_ref[...]).astype(BF16)
    y = jnp.dot(u, w_ref[...], preferred_element_type=F32)
    q_ref[0] = (y[:, 0:D_MODEL] * QSCALE).astype(BF16)
    k_ref[0] = y[:, D_MODEL:2 * D_MODEL].astype(BF16)
    v_ref[0] = y[:, 2 * D_MODEL:3 * D_MODEL].astype(BF16)


def _inproj_n(h3d, g, w, tm=512):
    b, s, _ = h3d.shape
    blk = pl.BlockSpec((1, tm, D_MODEL), lambda bi, ti: (bi, ti, 0))
    return pl.pallas_call(
        _inproj_n_kernel,
        grid=(b, s // tm),
        in_specs=[
            blk,
            _resident((1, D_MODEL), lambda bi, ti: (0, 0)),
            _resident((D_MODEL, 3 * D_MODEL), lambda bi, ti: (0, 0)),
        ],
        out_specs=[blk, blk, blk],
        out_shape=[jax.ShapeDtypeStruct((b, s, D_MODEL), BF16)] * 3,
        compiler_params=_cparams(2),
        name="inproj_na",
    )(h3d, g, w)


_SUM_ROWS = 16
_RING = 4
_LOOKAHEAD = 3
_PV_LAG = 1
_P_RING = 2
_FLASH_LANES = {"diff": 1024, "gqa": 1024}


def _flash_kernel(qT_ref, k_ref, vT_ref, *rest, variant, qb, nq, kb, nk, unroll, lam_init):
    if variant == "diff":
        lam_ref, gsub_ref, o_ref, s_ring, p_ring, acc_sc = rest
    else:
        o_ref, s_ring, p_ring, acc_sc = rest

    def padded_queries(qq):
        qT = qT_ref[0, :, pl.ds(pl.multiple_of(qq * qb, qb), qb)]
        if variant == "diff":
            zero = jnp.zeros((HEAD_DIM, qb), BF16)
            return jnp.concatenate(
                [jnp.concatenate([qT[0:64], zero], axis=0),
                 jnp.concatenate([zero, qT[64:128]], axis=0)], axis=1)
        parity = pl.program_id(1) % 2
        row_half = (lax.broadcasted_iota(jnp.int32, (128, qb), 0) >= HEAD_DIM).astype(jnp.int32)
        keep = row_half == parity
        parts = []
        for g in range(4):
            qg = qT[g * 64:(g + 1) * 64]
            parts.append(jnp.where(keep, jnp.concatenate([qg, qg], axis=0), jnp.zeros((), BF16)))
        return jnp.concatenate(parts, axis=1)

    lanes = s_ring.shape[2]
    dv = vT_ref.shape[1]
    ones_rows = jnp.ones((_SUM_ROWS, kb), BF16)

    def scores(qz, i):
        ks = i * kb if isinstance(i, int) else pl.multiple_of(i * kb, kb)
        return jnp.dot(k_ref[0, pl.ds(ks, kb), :], qz, preferred_element_type=F32)

    def weighted_values(i, p):
        ks = pl.multiple_of(i * kb, kb)
        v_ext = jnp.concatenate([vT_ref[0, :, pl.ds(ks, kb)], ones_rows], axis=0)
        return jnp.dot(v_ext, p, preferred_element_type=F32)

    def first_scores(qq):
        qz = padded_queries(qq)
        first = [scores(qz, t) for t in range(_LOOKAHEAD)]
        for t in range(_LOOKAHEAD):
            s_ring[t] = first[t]
        return tuple(jnp.max(f, axis=0, keepdims=True) for f in first)

    def stage(qz, i, slot, stats, score_qz=None, score_block=None):
        alphas, m_prev, maxes = stats
        m_new = jnp.maximum(m_prev, maxes[0])
        s_new = scores(qz if score_qz is None else score_qz,
                       i + _LOOKAHEAD if score_block is None else score_block)
        s_ring[(slot + _LOOKAHEAD) % _RING] = s_new
        maxes = maxes[1:] + (jnp.max(s_new, axis=0, keepdims=True),)
        acc_sc[...] = alphas[0] * acc_sc[...] + weighted_values(
            jnp.maximum(i - _PV_LAG, 0), p_ring[(slot - _PV_LAG) % _P_RING])
        p_ring[slot % _P_RING] = jnp.exp2(s_ring[slot % _RING] - m_new).astype(BF16)
        return alphas[1:] + (jnp.exp2(m_prev - m_new),), m_new, maxes

    n_body = nk // unroll

    def query_block(qq, maxes):
        qz = padded_queries(qq)
        for t in range(1, _PV_LAG + 1):
            p_ring[-t % _P_RING] = jnp.zeros(p_ring.shape[1:], BF16)
        acc_sc[...] = jnp.zeros(acc_sc.shape, F32)
        stats = ((jnp.ones((1, lanes), F32),) * _PV_LAG, jnp.full((1, lanes), NEG, F32), maxes)

        def body(ii, stats):
            for u in range(unroll):
                stats = stage(qz, unroll * ii + u, u, stats)
            return stats

        stats = lax.fori_loop(0, n_body - 1, body, stats)
        qz_next = padded_queries(jnp.minimum(qq + 1, nq - 1))
        for u in range(unroll):
            i_last = unroll * (n_body - 1) + u
            ahead = i_last + _LOOKAHEAD
            if ahead < nk:
                stats = stage(qz, i_last, u, stats)
            else:
                stats = stage(qz, i_last, u, stats, score_qz=qz_next, score_block=ahead - nk)
        acc = acc_sc[...]
        for t in range(_PV_LAG):
            blk = nk - _PV_LAG + t
            acc = stats[0][t] * acc + weighted_values(blk, p_ring[blk % _P_RING])

        o = acc[0:dv] / acc[dv:dv + 1]
        rows = pl.ds(pl.multiple_of(qq * qb, qb), qb)
        if variant == "diff":
            lv = lam_ref[...]
            lam = (jnp.exp(jnp.sum(lv[0:1] * lv[1:2], axis=1, keepdims=True))
                   - jnp.exp(jnp.sum(lv[2:3] * lv[3:4], axis=1, keepdims=True)) + lam_init)
            od = o[:, 0:qb] - lam * o[:, qb:2 * qb]
            ms = jnp.mean(od * od, axis=0, keepdims=True)
            od = od * lax.rsqrt(ms + EPS) * gsub_ref[...] * (1.0 - lam_init)
            o_ref[0, rows, :] = od.T.astype(BF16)
        else:
            stacked = jnp.concatenate([o[:, g * qb:(g + 1) * qb] for g in range(4)], axis=0)
            o_ref[0, rows, :] = stacked.T.astype(BF16)
        return stats[2]

    assert nk % _RING == 0 and _LOOKAHEAD <= unroll
    lax.fori_loop(0, nq, query_block, first_scores(0))


def _flash(qT, k, vT, variant, lam_vecs=None, gsub_b=None, lam_init=0.0, kb=256, unroll=8):
    b, _, s = qT.shape
    lanes = _FLASH_LANES[variant]
    if variant == "diff":
        qb, n_groups, r, dv, width = lanes // 2, 8, 128, 128, 128
        k_map = lambda bi, j: (bi, 0, j)
    else:
        qb, n_groups, r, dv, width = lanes // 4, 4, 256, 64, 256
        k_map = lambda bi, j: (bi, 0, j // 2)
    nk = s // kb
    assert unroll % _RING == 0 and nk % unroll == 0
    kern = functools.partial(_flash_kernel, variant=variant, qb=qb, nq=s // qb, kb=kb, nk=nk,
                             unroll=unroll, lam_init=lam_init)
    in_specs = [
        pl.BlockSpec((1, r, s), lambda bi, j: (bi, j, 0)),
        pl.BlockSpec((1, s, 128), k_map),
        pl.BlockSpec((1, dv, s), lambda bi, j: (bi, j, 0)),
    ]
    args = [qT, k, vT]
    if variant == "diff":
        in_specs += [
            _resident((4, HEAD_DIM), lambda bi, j: (0, 0)),
            _resident((128, qb), lambda bi, j: (0, 0)),
        ]
        args += [lam_vecs, gsub_b]
    return pl.pallas_call(
        kern,
        grid=(b, n_groups),
        in_specs=in_specs,
        out_specs=pl.BlockSpec((1, s, width), lambda bi, j: (bi, 0, j)),
        out_shape=jax.ShapeDtypeStruct((b, s, D_MODEL), BF16),
        scratch_shapes=[
            pltpu.VMEM((_RING, kb, lanes), F32),
            pltpu.VMEM((_P_RING, kb, lanes), BF16),
            pltpu.VMEM((dv + _SUM_ROWS, lanes), F32),
        ],
        compiler_params=_cparams(2),
        name="flash_" + variant,
    )(*args)


def _na_table_kernel(rpb_ref, pt_ref):
    cq = lax.broadcasted_iota(jnp.int32, (GRID_W, 128), 0)
    ck = lax.broadcasted_iota(jnp.int32, (GRID_W, 128), 1) & (GRID_W - 1)
    c0 = jnp.clip(cq - NA_WIN_W // 2, 0, GRID_W - NA_WIN_W)
    valid = (ck >= c0) & (ck < c0 + NA_WIN_W)
    for dr in range(2 * NA_WIN_H - 2):
        a = rpb_ref[0, dr:dr + 1, :]
        bb = rpb_ref[0, dr + 1:dr + 2, :]
        x = a + pltpu.roll(bb, GRID_W, axis=1)
        xb = jnp.broadcast_to(x, (GRID_W, 128))
        t = pltpu.roll(xb, 128 - (NA_WIN_W - 1), axis=1, stride=1, stride_axis=0)
        pt_ref[0, dr] = jnp.where(valid, t * LOG2E, NEG)


def _na_table(rpb_pad):
    nh = rpb_pad.shape[0]
    ndr = 2 * NA_WIN_H - 2
    return pl.pallas_call(
        _na_table_kernel,
        grid=(nh,),
        in_specs=[pl.BlockSpec((1, 16, 128), lambda h: (h, 0, 0))],
        out_specs=pl.BlockSpec((1, ndr, GRID_W, 128), lambda h: (h, 0, 0, 0)),
        out_shape=jax.ShapeDtypeStruct((nh, ndr, GRID_W, 128), F32),
        compiler_params=_cparams(1),
        name="na_table",
    )(rpb_pad)


_NA_ROWS = 4
_NA_BAND = _NA_ROWS + NA_WIN_H


def _na_kernel(q_ref, k0_ref, k1_ref, k2_ref, v0_ref, v1_ref, v2_ref, pt_ref, o_ref, kband, vband,
               *, n_steps):
    i = pl.program_id(1)
    blk = _NA_ROWS * GRID_W
    for n, (kr, vr) in enumerate(((k0_ref, v0_ref), (k1_ref, v1_ref), (k2_ref, v2_ref))):
        kband[n * blk:(n + 1) * blk, :] = kr[0]
        vband[n * blk:(n + 1) * blk, :] = vr[0]
    left = lax.broadcasted_iota(jnp.int32, (GRID_W, 128), 1) < HEAD_DIM
    nkeys = NA_WIN_H * GRID_W

    def row_body(t, carry):
        off = jnp.where(i == 0, 0, jnp.where(i == n_steps - 1, _NA_ROWS, t))
        sidx = jnp.where(i == 0, NA_WIN_H - 1 - t, jnp.where(i == n_steps - 1, 3 - t, 3))
        start = pl.multiple_of(off * GRID_W, GRID_W)
        qs = pl.multiple_of(t * GRID_W, GRID_W)
        n_pairs = D_MODEL // 128
        cols = [slice(j * 128, (j + 1) * 128) for j in range(n_pairs)]
        scores = []
        for j in range(n_pairs):
            qp = q_ref[0, pl.ds(qs, GRID_W), cols[j]]
            kp = kband[pl.ds(start, nkeys), cols[j]]
            for e in range(2):
                qz = jnp.where(left if e == 0 else jnp.logical_not(left), qp, jnp.zeros((), BF16))
                scores.append(lax.dot_general(qz, kp, _NT, preferred_element_type=F32))
        probs, denoms = [], []
        for h in range(2 * n_pairs):
            bias = jnp.concatenate([pt_ref[h, sidx + 2 * m] for m in range(NA_WIN_H // 2)], axis=1)
            s = scores[h] + bias
            p = jnp.exp2(s - jnp.max(s, axis=1, keepdims=True))
            denoms.append(jnp.sum(p, axis=1, keepdims=True))
            probs.append(p.astype(BF16))
        for j in range(n_pairs):
            vp = vband[pl.ds(start, nkeys), cols[j]]
            outs = [jnp.dot(probs[2 * j + e], vp, preferred_element_type=F32) / denoms[2 * j + e]
                    for e in range(2)]
            o_ref[0, pl.ds(qs, GRID_W), cols[j]] = jnp.where(left, outs[0], outs[1]).astype(BF16)
        return carry

    lax.fori_loop(0, _NA_ROWS, row_body, 0)


def _na_attention(q, k, v, pt):
    b, s, _ = q.shape
    blk = _NA_ROWS * GRID_W
    n_steps = s // blk
    last = n_steps - 3

    def band_spec(n):
        return pl.BlockSpec((1, blk, D_MODEL),
                            lambda bi, i: (bi, jnp.clip(i - 1, 0, last) + n, 0))

    kern = functools.partial(_na_kernel, n_steps=n_steps)
    return pl.pallas_call(
        kern,
        grid=(b, n_steps),
        in_specs=[
            pl.BlockSpec((1, blk, D_MODEL), lambda bi, i: (bi, i, 0)),
            band_spec(0), band_spec(1), band_spec(2),
            band_spec(0), band_spec(1), band_spec(2),
            _resident(pt.shape, lambda bi, i: (0, 0, 0, 0)),
        ],
        out_specs=pl.BlockSpec((1, blk, D_MODEL), lambda bi, i: (bi, i, 0)),
        out_shape=jax.ShapeDtypeStruct((b, s, D_MODEL), BF16),
        scratch_shapes=[
            pltpu.VMEM((_NA_BAND * GRID_W, D_MODEL), BF16),
            pltpu.VMEM((_NA_BAND * GRID_W, D_MODEL), BF16),
        ],
        compiler_params=_cparams(2),
        name="na_attention",
    )(q, k, k, k, v, v, v, pt)


def _angles(pos, dims, theta):
    inv = theta ** (-jnp.arange(0, dims, 2, dtype=F32) / dims)
    return pos[:, None] * inv[None, :]


def _diff_tables(s):
    ang = _angles(jnp.arange(s, dtype=F32), ROPE_DIMS, ROPE_THETA)
    c, sn = jnp.cos(ang).T, jnp.sin(ang).T
    ones = jnp.ones((HEAD_DIM - ROPE_DIMS, s), F32)
    ct = jnp.concatenate([c, c, ones], axis=0)
    st = jnp.concatenate([-sn, sn, 0.0 * ones], axis=0)
    return ct, st


def _axial_tables(s):
    t = jnp.arange(s)
    half = HEAD_DIM // 2
    ar = _angles((t // GRID_W).astype(F32), half, AXIAL_THETA)
    ac = _angles((t % GRID_W).astype(F32), half, AXIAL_THETA)
    cr, sr, cc, sc = jnp.cos(ar).T, jnp.sin(ar).T, jnp.cos(ac).T, jnp.sin(ac).T
    ct = jnp.concatenate([cr, cr, cc, cc], axis=0)
    st = jnp.concatenate([-sr, sr, -sc, sc], axis=0)
    return ct, st


def _col_gain(g, tm):
    return jnp.broadcast_to(g.astype(F32)[:, None], (g.shape[0], tm))


def kernel(x, norm_g, ffn1_wg, ffn1_wu, ffn1_wd, ffn2_wg, ffn2_wu, ffn2_wd,
           diff_w_in, diff_w_out, diff_lambda, diff_subln,
           na_w_in, na_w_out, na_rpb,
           gqa_w_in, gqa_w_out, gqa_qk_norm):
    b, s, d = x.shape
    n = b * s
    tm = 512
    h = x.reshape(n, d)
    ones_gain = jnp.ones((HEAD_DIM, tm), F32)

    for i in range(DEPTH):
        g = norm_g[i]
        h = _ffn(h, g[0:2], ffn1_wg[i].astype(BF16), ffn1_wu[i].astype(BF16), ffn1_wd[i].astype(BF16))
        kind, j = i % N_MIXERS, i // N_MIXERS
        h3 = h.reshape(b, s, d)
        if kind == 0:
            w = diff_w_in[j]
            wqT = w[:, 0:d].T.astype(BF16)
            wkT = w[:, d:2 * d].T.astype(BF16)
            wvT = w[:, 2 * d:3 * d].T.astype(BF16)
            ct, st = _diff_tables(s)
            qT, k, vT = _inproj_t(h3, g[2:3], wqT, wkT, wvT, ct, st, ones_gain, ones_gain, "diff", tm)
            lam_init = 0.8 - 0.6 * math.exp(-0.3 * i)
            gsub_b = _col_gain(diff_subln[j], _FLASH_LANES["diff"] // 2)
            o = _flash(qT, k, vT, "diff", diff_lambda[j].astype(F32), gsub_b, lam_init, kb=512)
            w_out = diff_w_out[j]
        elif kind == 1:
            q, k, v = _inproj_n(h3, g[2:3], na_w_in[j].astype(BF16), tm)
            rpb = na_rpb[j].astype(F32)
            rpb_pad = jnp.zeros((rpb.shape[0], 16, 128), F32).at[:, :rpb.shape[1], :rpb.shape[2]].set(rpb)
            pt = _na_table(rpb_pad)
            o = _na_attention(q, k, v, pt)
            w_out = na_w_out[j]
        else:
            w = gqa_w_in[j]
            nq = d
            nkv = (w.shape[1] - nq) // 2
            wqT = w[:, 0:nq].T.astype(BF16)
            wkT = w[:, nq:nq + nkv].T.astype(BF16)
            wvT = w[:, nq + nkv:].T.astype(BF16)
            ct, st = _axial_tables(s)
            gq_b = _col_gain(gqa_qk_norm[j, 0], tm)
            gk_b = _col_gain(gqa_qk_norm[j, 1], tm)
            qT, k, vT = _inproj_t(h3, g[2:3], wqT, wkT, wvT, ct, st, gq_b, gk_b, "gqa", tm)
            o = _flash(qT, k, vT, "gqa", kb=512)
            w_out = gqa_w_out[j]
        h = _outproj_ffn(o.reshape(n, d), h, g[3:6], w_out.astype(BF16),
                         ffn2_wg[i].astype(BF16), ffn2_wu[i].astype(BF16), ffn2_wd[i].astype(BF16), tm)
    return h.reshape(b, s, d)
```

```python
import functools
import math

import jax
import jax.numpy as jnp
from jax import lax
from jax.experimental import pallas as pl
from jax.experimental.pallas import tpu as pltpu

D_MODEL = 1024
SEQ = 8192
DEPTH = 4
N_MIXERS = 3
GRID_W = 64
EPS = 1e-6
D_FF = 2816
FFN_RES = 0.5
HEAD_DIM = 64
ROPE_THETA = 500000.0
ROPE_DIMS = 16
NA_WIN_H = 8
NA_WIN_W = 16
AXIAL_THETA = 10000.0

LOG2E = 1.4426950408889634
QSCALE = (HEAD_DIM ** -0.5) * LOG2E
NEG = -1e30

VMEM_LIMIT_BYTES = 56 * 1024 * 1024

BF16 = jnp.bfloat16
F32 = jnp.float32

_NT = (((1,), (1,)), ((), ()))


def _cparams(n_axes):
    return pltpu.CompilerParams(
        dimension_semantics=("parallel",) * n_axes,
        vmem_limit_bytes=VMEM_LIMIT_BYTES,
    )


def _resident(shape, index_map):
    return pl.BlockSpec(shape, index_map, pipeline_mode=pl.Buffered(1))


def _rms_rows(x, g):
    ms = jnp.mean(x * x, axis=-1, keepdims=True)
    return x * lax.rsqrt(ms + EPS) * g


def _ffn_block(x, g_pre, g_post, wg_ref, wu_ref, wd_ref):
    xn = _rms_rows(x, g_pre).astype(BF16)
    gate = jnp.dot(xn, wg_ref[...], preferred_element_type=F32)
    up = jnp.dot(xn, wu_ref[...], preferred_element_type=F32)
    act = (gate * jax.nn.sigmoid(gate) * up).astype(BF16)
    y = jnp.dot(act, wd_ref[...], preferred_element_type=F32)
    return x + FFN_RES * _rms_rows(y, g_post)


def _ffn_kernel(x_ref, g_ref, wg_ref, wu_ref, wd_ref, o_ref):
    g = g_ref[...]
    o_ref[...] = _ffn_block(x_ref[...], g[0:1], g[1:2], wg_ref, wu_ref, wd_ref)


def _outproj_ffn_kernel(o_ref, h_ref, g_ref, wo_ref, wg_ref, wu_ref, wd_ref, out_ref):
    g = g_ref[...]
    m = jnp.dot(o_ref[...], wo_ref[...], preferred_element_type=F32)
    h1 = h_ref[...] + _rms_rows(m, g[0:1])
    out_ref[...] = _ffn_block(h1, g[1:2], g[2:3], wg_ref, wu_ref, wd_ref)


def _outproj_ffn(o2d, h2d, g3, wo, wg, wu, wd, tm=512):
    n = h2d.shape[0]
    blk = pl.BlockSpec((tm, D_MODEL), lambda i: (i, 0))
    return pl.pallas_call(
        _outproj_ffn_kernel,
        grid=(n // tm,),
        in_specs=[
            blk, blk,
            _resident((3, D_MODEL), lambda i: (0, 0)),
            _resident((D_MODEL, D_MODEL), lambda i: (0, 0)),
            _resident((D_MODEL, D_FF), lambda i: (0, 0)),
            _resident((D_MODEL, D_FF), lambda i: (0, 0)),
            _resident((D_FF, D_MODEL), lambda i: (0, 0)),
        ],
        out_specs=blk,
        out_shape=jax.ShapeDtypeStruct(h2d.shape, F32),
        compiler_params=_cparams(1),
        name="outproj_ffn",
    )(o2d, h2d, g3, wo, wg, wu, wd)


def _ffn(h2d, g2, wg, wu, wd, tm=512):
    n = h2d.shape[0]
    return pl.pallas_call(
        _ffn_kernel,
        grid=(n // tm,),
        in_specs=[
            pl.BlockSpec((tm, D_MODEL), lambda i: (i, 0)),
            _resident((2, D_MODEL), lambda i: (0, 0)),
            _resident((D_MODEL, D_FF), lambda i: (0, 0)),
            _resident((D_MODEL, D_FF), lambda i: (0, 0)),
            _resident((D_FF, D_MODEL), lambda i: (0, 0)),
        ],
        out_specs=pl.BlockSpec((tm, D_MODEL), lambda i: (i, 0)),
        out_shape=jax.ShapeDtypeStruct(h2d.shape, F32),
        compiler_params=_cparams(1),
        name="ffn",
    )(h2d, g2, wg, wu, wd)


def _swap_halves(x, half):
    blocks = []
    r = 0
    n = x.shape[0]
    while r + 2 * half <= n:
        blocks += [x[r + half:r + 2 * half], x[r:r + half]]
        r += 2 * half
    return jnp.concatenate(blocks, axis=0)


def _inproj_t_kernel(h_ref, g_ref, wqT_ref, wkT_ref, wvT_ref, ct_ref, st_ref, gq_ref, gk_ref,
                     qT_ref, k_ref, vT_ref, *, variant):
    u = _rms_rows(h_ref[0], g_ref[...]).astype(BF16)
    ct = ct_ref[...]
    st = st_ref[...]

    def head(x, gain, scale):
        if variant == "gqa":
            ms = jnp.mean(x * x, axis=0, keepdims=True)
            x = x * lax.rsqrt(ms + EPS) * gain
            xs = _swap_halves(x, 16)
        else:
            xs = jnp.concatenate([x[8:16], x[0:8], x[16:64]], axis=0)
        x = x * ct + xs * st
        if scale is not None:
            x = x * scale
        return x

    gq = gq_ref[...]
    gk = gk_ref[...]

    qT = lax.dot_general(wqT_ref[...], u, _NT, preferred_element_type=F32)
    for hh in range(qT.shape[0] // HEAD_DIM):
        sl = slice(hh * HEAD_DIM, (hh + 1) * HEAD_DIM)
        qT_ref[0, sl, :] = head(qT[sl], gq, QSCALE).astype(BF16)

    kT = lax.dot_general(wkT_ref[...], u, _NT, preferred_element_type=F32)
    for pp in range(kT.shape[0] // 128):
        pair = jnp.concatenate(
            [head(kT[pp * 128 + e * 64: pp * 128 + (e + 1) * 64], gk, None) for e in range(2)], axis=0)
        k_ref[0, :, pp * 128:(pp + 1) * 128] = pair.T.astype(BF16)

    vT = lax.dot_general(wvT_ref[...], u, _NT, preferred_element_type=F32)
    vT_ref[0] = vT.astype(BF16)


def _inproj_t(h3d, g, wqT, wkT, wvT, ct, st, gq_b, gk_b, variant, tm=512):
    b, s, _ = h3d.shape
    nq, nk, nv = wqT.shape[0], wkT.shape[0], wvT.shape[0]
    kern = functools.partial(_inproj_t_kernel, variant=variant)
    return pl.pallas_call(
        kern,
        grid=(b, s // tm),
        in_specs=[
            pl.BlockSpec((1, tm, D_MODEL), lambda bi, ti: (bi, ti, 0)),
            _resident((1, D_MODEL), lambda bi, ti: (0, 0)),
            _resident((nq, D_MODEL), lambda bi, ti: (0, 0)),
            _resident((nk, D_MODEL), lambda bi, ti: (0, 0)),
            _resident((nv, D_MODEL), lambda bi, ti: (0, 0)),
            pl.BlockSpec((HEAD_DIM, tm), lambda bi, ti: (0, ti)),
            pl.BlockSpec((HEAD_DIM, tm), lambda bi, ti: (0, ti)),
            _resident((HEAD_DIM, tm), lambda bi, ti: (0, 0)),
            _resident((HEAD_DIM, tm), lambda bi, ti: (0, 0)),
        ],
        out_specs=[
            pl.BlockSpec((1, nq, tm), lambda bi, ti: (bi, 0, ti)),
            pl.BlockSpec((1, tm, nk), lambda bi, ti: (bi, ti, 0)),
            pl.BlockSpec((1, nv, tm), lambda bi, ti: (bi, 0, ti)),
        ],
        out_shape=[
            jax.ShapeDtypeStruct((b, nq, s), BF16),
            jax.ShapeDtypeStruct((b, s, nk), BF16),
            jax.ShapeDtypeStruct((b, nv, s), BF16),
        ],
        compiler_params=_cparams(2),
        name="inproj_" + variant,
    )(h3d, g, wqT, wkT, wvT, ct, st, gq_b, gk_b)


def _inproj_n_kernel(h_ref, g_ref, w_ref, q_ref, k_ref, v_ref):
    u = _rms_rows(h_ref[0], g_ref[...]).astype(BF16)
    y = jnp.dot(u, w_ref[...], preferred_element_type=F32)
    q_ref[0] = (y[:, 0:D_MODEL] * QSCALE).astype(BF16)
    k_ref[0] = y[:, D_MODEL:2 * D_MODEL].astype(BF16)
    v_ref[0] = y[:, 2 * D_MODEL:3 * D_MODEL].astype(BF16)


def _inproj_n(h3d, g, w, tm=512):
    b, s, _ = h3d.shape
    blk = pl.BlockSpec((1, tm, D_MODEL), lambda bi, ti: (bi, ti, 0))
    return pl.pallas_call(
        _inproj_n_kernel,
        grid=(b, s // tm),
        in_specs=[
            blk,
            _resident((1, D_MODEL), lambda bi, ti: (0, 0)),
            _resident((D_MODEL, 3 * D_MODEL), lambda bi, ti: (0, 0)),
        ],
        out_specs=[blk, blk, blk],
        out_shape=[jax.ShapeDtypeStruct((b, s, D_MODEL), BF16)] * 3,
        compiler_params=_cparams(2),
        name="inproj_na",
    )(h3d, g, w)


_SUM_ROWS = 16
_MIN_PV_ROWS = 128
_RING = 4
_LOOKAHEAD = 3
_PV_LAG = 1
_P_RING = 2
_FLASH_LANES = {"diff": 1024, "gqa": 1024}


def _flash_kernel(qT_ref, k_ref, vT_ref, *rest, variant, qb, nq, kb, nk, unroll, lam_init):
    if variant == "diff":
        lam_ref, gsub_ref, o_ref, s_ring, p_ring, acc_sc = rest
    else:
        o_ref, s_ring, p_ring, acc_sc = rest

    def padded_queries(qq):
        qT = qT_ref[0, :, pl.ds(pl.multiple_of(qq * qb, qb), qb)]
        if variant == "diff":
            zero = jnp.zeros((HEAD_DIM, qb), BF16)
            return jnp.concatenate(
                [jnp.concatenate([qT[0:64], zero], axis=0),
                 jnp.concatenate([zero, qT[64:128]], axis=0)], axis=1)
        parity = pl.program_id(1) % 2
        row_half = (lax.broadcasted_iota(jnp.int32, (128, qb), 0) >= HEAD_DIM).astype(jnp.int32)
        keep = row_half == parity
        parts = []
        for g in range(4):
            qg = qT[g * 64:(g + 1) * 64]
            parts.append(jnp.where(keep, jnp.concatenate([qg, qg], axis=0), jnp.zeros((), BF16)))
        return jnp.concatenate(parts, axis=1)

    lanes = s_ring.shape[2]
    dv = vT_ref.shape[1]
    ones_rows = jnp.ones((max(_SUM_ROWS, _MIN_PV_ROWS - dv), kb), BF16)

    def scores(qz, i):
        ks = i * kb if isinstance(i, int) else pl.multiple_of(i * kb, kb)
        return jnp.dot(k_ref[0, pl.ds(ks, kb), :], qz, preferred_element_type=F32)

    def weighted_values(i, p):
        ks = pl.multiple_of(i * kb, kb)
        v_ext = jnp.concatenate([vT_ref[0, :, pl.ds(ks, kb)], ones_rows], axis=0)
        return jnp.dot(v_ext, p, preferred_element_type=F32)[0:dv + _SUM_ROWS]

    def first_scores(qq):
        qz = padded_queries(qq)
        first = [scores(qz, t) for t in range(_LOOKAHEAD)]
        for t in range(_LOOKAHEAD):
            s_ring[t] = first[t]
        return tuple(jnp.max(f, axis=0, keepdims=True) for f in first)

    def stage(qz, i, slot, stats, score_qz=None, score_block=None):
        alphas, m_prev, maxes = stats
        m_new = jnp.maximum(m_prev, maxes[0])
        s_new = scores(qz if score_qz is None else score_qz,
                       i + _LOOKAHEAD if score_block is None else score_block)
        s_ring[(slot + _LOOKAHEAD) % _RING] = s_new
        maxes = maxes[1:] + (jnp.max(s_new, axis=0, keepdims=True),)
        acc_sc[...] = alphas[0] * acc_sc[...] + weighted_values(
            jnp.maximum(i - _PV_LAG, 0), p_ring[(slot - _PV_LAG) % _P_RING])
        p_ring[slot % _P_RING] = jnp.exp2(s_ring[slot % _RING] - m_new).astype(BF16)
        return alphas[1:] + (jnp.exp2(m_prev - m_new),), m_new, maxes

    n_body = nk // unroll

    def query_block(qq, maxes):
        qz = padded_queries(qq)
        for t in range(1, _PV_LAG + 1):
            p_ring[-t % _P_RING] = jnp.zeros(p_ring.shape[1:], BF16)
        acc_sc[...] = jnp.zeros(acc_sc.shape, F32)
        stats = ((jnp.ones((1, lanes), F32),) * _PV_LAG, jnp.full((1, lanes), NEG, F32), maxes)

        def body(ii, stats):
            for u in range(unroll):
                stats = stage(qz, unroll * ii + u, u, stats)
            return stats

        stats = lax.fori_loop(0, n_body - 1, body, stats)
        qz_next = padded_queries(jnp.minimum(qq + 1, nq - 1))
        for u in range(unroll):
            i_last = unroll * (n_body - 1) + u
            ahead = i_last + _LOOKAHEAD
            if ahead < nk:
                stats = stage(qz, i_last, u, stats)
            else:
                stats = stage(qz, i_last, u, stats, score_qz=qz_next, score_block=ahead - nk)
        acc = acc_sc[...]
        for t in range(_PV_LAG):
            blk = nk - _PV_LAG + t
            acc = stats[0][t] * acc + weighted_values(blk, p_ring[blk % _P_RING])

        o = acc[0:dv] / acc[dv:dv + 1]
        rows = pl.ds(pl.multiple_of(qq * qb, qb), qb)
        if variant == "diff":
            lv = lam_ref[...]
            lam = (jnp.exp(jnp.sum(lv[0:1] * lv[1:2], axis=1, keepdims=True))
                   - jnp.exp(jnp.sum(lv[2:3] * lv[3:4], axis=1, keepdims=True)) + lam_init)
            od = o[:, 0:qb] - lam * o[:, qb:2 * qb]
            ms = jnp.mean(od * od, axis=0, keepdims=True)
            od = od * lax.rsqrt(ms + EPS) * gsub_ref[...] * (1.0 - lam_init)
            o_ref[0, rows, :] = od.T.astype(BF16)
        else:
            stacked = jnp.concatenate([o[:, g * qb:(g + 1) * qb] for g in range(4)], axis=0)
            o_ref[0, rows, :] = stacked.T.astype(BF16)
        return stats[2]

    assert nk % _RING == 0 and _LOOKAHEAD <= unroll
    lax.fori_loop(0, nq, query_block, first_scores(0))


def _flash(qT, k, vT, variant, lam_vecs=None, gsub_b=None, lam_init=0.0, kb=256, unroll=8):
    b, _, s = qT.shape
    lanes = _FLASH_LANES[variant]
    if variant == "diff":
        qb, n_groups, r, dv, width = lanes // 2, 8, 128, 128, 128
        k_map = lambda bi, j: (bi, 0, j)
    else:
        qb, n_groups, r, dv, width = lanes // 4, 4, 256, 64, 256
        k_map = lambda bi, j: (bi, 0, j // 2)
    nk = s // kb
    assert unroll % _RING == 0 and nk % unroll == 0
    kern = functools.partial(_flash_kernel, variant=variant, qb=qb, nq=s // qb, kb=kb, nk=nk,
                             unroll=unroll, lam_init=lam_init)
    in_specs = [
        pl.BlockSpec((1, r, s), lambda bi, j: (bi, j, 0)),
        pl.BlockSpec((1, s, 128), k_map),
        pl.BlockSpec((1, dv, s), lambda bi, j: (bi, j, 0)),
    ]
    args = [qT, k, vT]
    if variant == "diff":
        in_specs += [
            _resident((4, HEAD_DIM), lambda bi, j: (0, 0)),
            _resident((128, qb), lambda bi, j: (0, 0)),
        ]
        args += [lam_vecs, gsub_b]
    return pl.pallas_call(
        kern,
        grid=(b, n_groups),
        in_specs=in_specs,
        out_specs=pl.BlockSpec((1, s, width), lambda bi, j: (bi, 0, j)),
        out_shape=jax.ShapeDtypeStruct((b, s, D_MODEL), BF16),
        scratch_shapes=[
            pltpu.VMEM((_RING, kb, lanes), F32),
            pltpu.VMEM((_P_RING, kb, lanes), BF16),
            pltpu.VMEM((dv + _SUM_ROWS, lanes), F32),
        ],
        compiler_params=_cparams(2),
        name="flash_" + variant,
    )(*args)


def _na_table_kernel(rpb_ref, pt_ref):
    cq = lax.broadcasted_iota(jnp.int32, (GRID_W, 128), 0)
    ck = lax.broadcasted_iota(jnp.int32, (GRID_W, 128), 1) & (GRID_W - 1)
    c0 = jnp.clip(cq - NA_WIN_W // 2, 0, GRID_W - NA_WIN_W)
    valid = (ck >= c0) & (ck < c0 + NA_WIN_W)
    for dr in range(2 * NA_WIN_H - 2):
        a = rpb_ref[0, dr:dr + 1, :]
        bb = rpb_ref[0, dr + 1:dr + 2, :]
        x = a + pltpu.roll(bb, GRID_W, axis=1)
        xb = jnp.broadcast_to(x, (GRID_W, 128))
        t = pltpu.roll(xb, 128 - (NA_WIN_W - 1), axis=1, stride=1, stride_axis=0)
        pt_ref[0, dr] = jnp.where(valid, t * LOG2E, NEG)


def _na_table(rpb_pad):
    nh = rpb_pad.shape[0]
    ndr = 2 * NA_WIN_H - 2
    return pl.pallas_call(
        _na_table_kernel,
        grid=(nh,),
        in_specs=[pl.BlockSpec((1, 16, 128), lambda h: (h, 0, 0))],
        out_specs=pl.BlockSpec((1, ndr, GRID_W, 128), lambda h: (h, 0, 0, 0)),
        out_shape=jax.ShapeDtypeStruct((nh, ndr, GRID_W, 128), F32),
        compiler_params=_cparams(1),
        name="na_table",
    )(rpb_pad)


_NA_ROWS = 4
_NA_BAND = _NA_ROWS + NA_WIN_H


def _na_kernel(q_ref, k0_ref, k1_ref, k2_ref, v0_ref, v1_ref, v2_ref, pt_ref, o_ref, kband, vband,
               *, n_steps):
    i = pl.program_id(1)
    blk = _NA_ROWS * GRID_W
    for n, (kr, vr) in enumerate(((k0_ref, v0_ref), (k1_ref, v1_ref), (k2_ref, v2_ref))):
        kband[n * blk:(n + 1) * blk, :] = kr[0]
        vband[n * blk:(n + 1) * blk, :] = vr[0]
    left = lax.broadcasted_iota(jnp.int32, (GRID_W, 128), 1) < HEAD_DIM
    nkeys = NA_WIN_H * GRID_W

    def row_body(t, carry):
        off = jnp.where(i == 0, 0, jnp.where(i == n_steps - 1, _NA_ROWS, t))
        sidx = jnp.where(i == 0, NA_WIN_H - 1 - t, jnp.where(i == n_steps - 1, 3 - t, 3))
        start = pl.multiple_of(off * GRID_W, GRID_W)
        qs = pl.multiple_of(t * GRID_W, GRID_W)
        n_pairs = D_MODEL // 128
        cols = [slice(j * 128, (j + 1) * 128) for j in range(n_pairs)]
        scores = []
        for j in range(n_pairs):
            qp = q_ref[0, pl.ds(qs, GRID_W), cols[j]]
            kp = kband[pl.ds(start, nkeys), cols[j]]
            for e in range(2):
                qz = jnp.where(left if e == 0 else jnp.logical_not(left), qp, jnp.zeros((), BF16))
                scores.append(lax.dot_general(qz, kp, _NT, preferred_element_type=F32))
        probs, denoms = [], []
        for h in range(2 * n_pairs):
            bias = jnp.concatenate([pt_ref[h, sidx + 2 * m] for m in range(NA_WIN_H // 2)], axis=1)
            s = scores[h] + bias
            p = jnp.exp2(s - jnp.max(s, axis=1, keepdims=True))
            denoms.append(jnp.sum(p, axis=1, keepdims=True))
            probs.append(p.astype(BF16))
        for j in range(n_pairs):
            vp = vband[pl.ds(start, nkeys), cols[j]]
            outs = [jnp.dot(probs[2 * j + e], vp, preferred_element_type=F32) / denoms[2 * j + e]
                    for e in range(2)]
            o_ref[0, pl.ds(qs, GRID_W), cols[j]] = jnp.where(left, outs[0], outs[1]).astype(BF16)
        return carry

    lax.fori_loop(0, _NA_ROWS, row_body, 0)


def _na_attention(q, k, v, pt):
    b, s, _ = q.shape
    blk = _NA_ROWS * GRID_W
    n_steps = s // blk
    last = n_steps - 3

    def band_spec(n):
        return pl.BlockSpec((1, blk, D_MODEL),
                            lambda bi, i: (bi, jnp.clip(i - 1, 0, last) + n, 0))

    kern = functools.partial(_na_kernel, n_steps=n_steps)
    return pl.pallas_call(
        kern,
        grid=(b, n_steps),
        in_specs=[
            pl.BlockSpec((1, blk, D_MODEL), lambda bi, i: (bi, i, 0)),
            band_spec(0), band_spec(1), band_spec(2),
            band_spec(0), band_spec(1), band_spec(2),
            _resident(pt.shape, lambda bi, i: (0, 0, 0, 0)),
        ],
        out_specs=pl.BlockSpec((1, blk, D_MODEL), lambda bi, i: (bi, i, 0)),
        out_shape=jax.ShapeDtypeStruct((b, s, D_MODEL), BF16),
        scratch_shapes=[
            pltpu.VMEM((_NA_BAND * GRID_W, D_MODEL), BF16),
            pltpu.VMEM((_NA_BAND * GRID_W, D_MODEL), BF16),
        ],
        compiler_params=_cparams(2),
        name="na_attention",
    )(q, k, k, k, v, v, v, pt)


def _angles(pos, dims, theta):
    inv = theta ** (-jnp.arange(0, dims, 2, dtype=F32) / dims)
    return pos[:, None] * inv[None, :]


def _diff_tables(s):
    ang = _angles(jnp.arange(s, dtype=F32), ROPE_DIMS, ROPE_THETA)
    c, sn = jnp.cos(ang).T, jnp.sin(ang).T
    ones = jnp.ones((HEAD_DIM - ROPE_DIMS, s), F32)
    ct = jnp.concatenate([c, c, ones], axis=0)
    st = jnp.concatenate([-sn, sn, 0.0 * ones], axis=0)
    return ct, st


def _axial_tables(s):
    t = jnp.arange(s)
    half = HEAD_DIM // 2
    ar = _angles((t // GRID_W).astype(F32), half, AXIAL_THETA)
    ac = _angles((t % GRID_W).astype(F32), half, AXIAL_THETA)
    cr, sr, cc, sc = jnp.cos(ar).T, jnp.sin(ar).T, jnp.cos(ac).T, jnp.sin(ac).T
    ct = jnp.concatenate([cr, cr, cc, cc], axis=0)
    st = jnp.concatenate([-sr, sr, -sc, sc], axis=0)
    return ct, st


def _col_gain(g, tm):
    return jnp.broadcast_to(g.astype(F32)[:, None], (g.shape[0], tm))


def kernel(x, norm_g, ffn1_wg, ffn1_wu, ffn1_wd, ffn2_wg, ffn2_wu, ffn2_wd,
           diff_w_in, diff_w_out, diff_lambda, diff_subln,
           na_w_in, na_w_out, na_rpb,
           gqa_w_in, gqa_w_out, gqa_qk_norm):
    b, s, d = x.shape
    n = b * s
    tm = 512
    h = x.reshape(n, d)
    ones_gain = jnp.ones((HEAD_DIM, tm), F32)

    for i in range(DEPTH):
        g = norm_g[i]
        h = _ffn(h, g[0:2], ffn1_wg[i].astype(BF16), ffn1_wu[i].astype(BF16), ffn1_wd[i].astype(BF16))
        kind, j = i % N_MIXERS, i // N_MIXERS
        h3 = h.reshape(b, s, d)
        if kind == 0:
            w = diff_w_in[j]
            wqT = w[:, 0:d].T.astype(BF16)
            wkT = w[:, d:2 * d].T.astype(BF16)
            wvT = w[:, 2 * d:3 * d].T.astype(BF16)
            ct, st = _diff_tables(s)
            qT, k, vT = _inproj_t(h3, g[2:3], wqT, wkT, wvT, ct, st, ones_gain, ones_gain, "diff", tm)
            lam_init = 0.8 - 0.6 * math.exp(-0.3 * i)
            gsub_b = _col_gain(diff_subln[j], _FLASH_LANES["diff"] // 2)
            o = _flash(qT, k, vT, "diff", diff_lambda[j].astype(F32), gsub_b, lam_init, kb=512)
            w_out = diff_w_out[j]
        elif kind == 1:
            q, k, v = _inproj_n(h3, g[2:3], na_w_in[j].astype(BF16), tm)
            rpb = na_rpb[j].astype(F32)
            rpb_pad = jnp.zeros((rpb.shape[0], 16, 128), F32).at[:, :rpb.shape[1], :rpb.shape[2]].set(rpb)
            pt = _na_table(rpb_pad)
            o = _na_attention(q, k, v, pt)
            w_out = na_w_out[j]
        else:
            w = gqa_w_in[j]
            nq = d
            nkv = (w.shape[1] - nq) // 2
            wqT = w[:, 0:nq].T.astype(BF16)
            wkT = w[:, nq:nq + nkv].T.astype(BF16)
            wvT = w[:, nq + nkv:].T.astype(BF16)
            ct, st = _axial_tables(s)
            gq_b = _col_gain(gqa_qk_norm[j, 0], tm)
            gk_b = _col_gain(gqa_qk_norm[j, 1], tm)
            qT, k, vT = _inproj_t(h3, g[2:3], wqT, wkT, wvT, ct, st, gq_b, gk_b, "gqa", tm)
            o = _flash(qT, k, vT, "gqa", kb=512)
            w_out = gqa_w_out[j]
        h = _outproj_ffn(o.reshape(n, d), h, g[3:6], w_out.astype(BF16),
                         ffn2_wg[i].astype(BF16), ffn2_wu[i].astype(BF16), ffn2_wd[i].astype(BF16), tm)
    return h.reshape(b, s, d)
```

```python
import functools
import math

import jax
import jax.numpy as jnp
from jax import lax
from jax.experimental import pallas as pl
from jax.experimental.pallas import tpu as pltpu

D_MODEL = 1024
SEQ = 8192
DEPTH = 4
N_MIXERS = 3
GRID_W = 64
EPS = 1e-6
D_FF = 2816
FFN_RES = 0.5
HEAD_DIM = 64
ROPE_THETA = 500000.0
ROPE_DIMS = 16
NA_WIN_H = 8
NA_WIN_W = 16
AXIAL_THETA = 10000.0

LOG2E = 1.4426950408889634
QSCALE = (HEAD_DIM ** -0.5) * LOG2E
NEG = -1e30

VMEM_LIMIT_BYTES = 56 * 1024 * 1024

BF16 = jnp.bfloat16
F32 = jnp.float32

_NT = (((1,), (1,)), ((), ()))


def _cparams(n_axes):
    return pltpu.CompilerParams(
        dimension_semantics=("parallel",) * n_axes,
        vmem_limit_bytes=VMEM_LIMIT_BYTES,
    )


def _resident(shape, index_map):
    return pl.BlockSpec(shape, index_map, pipeline_mode=pl.Buffered(1))


def _rms_rows(x, g):
    ms = jnp.mean(x * x, axis=-1, keepdims=True)
    return x * lax.rsqrt(ms + EPS) * g


def _ffn_block(x, g_pre, g_post, wg_ref, wu_ref, wd_ref):
    xn = _rms_rows(x, g_pre).astype(BF16)
    gate = jnp.dot(xn, wg_ref[...], preferred_element_type=F32)
    up = jnp.dot(xn, wu_ref[...], preferred_element_type=F32)
    act = (gate * jax.nn.sigmoid(gate) * up).astype(BF16)
    y = jnp.dot(act, wd_ref[...], preferred_element_type=F32)
    return x + FFN_RES * _rms_rows(y, g_post)


def _ffn_kernel(x_ref, g_ref, wg_ref, wu_ref, wd_ref, o_ref):
    g = g_ref[...]
    o_ref[...] = _ffn_block(x_ref[...], g[0:1], g[1:2], wg_ref, wu_ref, wd_ref)


def _outproj_ffn_kernel(o_ref, h_ref, g_ref, wo_ref, wg_ref, wu_ref, wd_ref, out_ref):
    g = g_ref[...]
    m = jnp.dot(o_ref[...], wo_ref[...], preferred_element_type=F32)
    h1 = h_ref[...] + _rms_rows(m, g[0:1])
    out_ref[...] = _ffn_block(h1, g[1:2], g[2:3], wg_ref, wu_ref, wd_ref)


def _outproj_ffn(o2d, h2d, g3, wo, wg, wu, wd, tm=512):
    n = h2d.shape[0]
    blk = pl.BlockSpec((tm, D_MODEL), lambda i: (i, 0))
    return pl.pallas_call(
        _outproj_ffn_kernel,
        grid=(n // tm,),
        in_specs=[
            blk, blk,
            _resident((3, D_MODEL), lambda i: (0, 0)),
            _resident((D_MODEL, D_MODEL), lambda i: (0, 0)),
            _resident((D_MODEL, D_FF), lambda i: (0, 0)),
            _resident((D_MODEL, D_FF), lambda i: (0, 0)),
            _resident((D_FF, D_MODEL), lambda i: (0, 0)),
        ],
        out_specs=blk,
        out_shape=jax.ShapeDtypeStruct(h2d.shape, F32),
        compiler_params=_cparams(1),
        name="outproj_ffn",
    )(o2d, h2d, g3, wo, wg, wu, wd)


def _ffn(h2d, g2, wg, wu, wd, tm=512):
    n = h2d.shape[0]
    return pl.pallas_call(
        _ffn_kernel,
        grid=(n // tm,),
        in_specs=[
            pl.BlockSpec((tm, D_MODEL), lambda i: (i, 0)),
            _resident((2, D_MODEL), lambda i: (0, 0)),
            _resident((D_MODEL, D_FF), lambda i: (0, 0)),
            _resident((D_MODEL, D_FF), lambda i: (0, 0)),
            _resident((D_FF, D_MODEL), lambda i: (0, 0)),
        ],
        out_specs=pl.BlockSpec((tm, D_MODEL), lambda i: (i, 0)),
        out_shape=jax.ShapeDtypeStruct(h2d.shape, F32),
        compiler_params=_cparams(1),
        name="ffn",
    )(h2d, g2, wg, wu, wd)


def _swap_halves(x, half):
    blocks = []
    r = 0
    n = x.shape[0]
    while r + 2 * half <= n:
        blocks += [x[r + half:r + 2 * half], x[r:r + half]]
        r += 2 * half
    return jnp.concatenate(blocks, axis=0)


def _inproj_t_kernel(h_ref, g_ref, wqT_ref, wkT_ref, wvT_ref, ct_ref, st_ref, gq_ref, gk_ref,
                     qT_ref, k_ref, vT_ref, *, variant):
    u = _rms_rows(h_ref[0], g_ref[...]).astype(BF16)
    ct = ct_ref[...]
    st = st_ref[...]

    def head(x, gain, scale):
        if variant == "gqa":
            ms = jnp.mean(x * x, axis=0, keepdims=True)
            x = x * lax.rsqrt(ms + EPS) * gain
            xs = _swap_halves(x, 16)
        else:
            xs = jnp.concatenate([x[8:16], x[0:8], x[16:64]], axis=0)
        x = x * ct + xs * st
        if scale is not None:
            x = x * scale
        return x

    gq = gq_ref[...]
    gk = gk_ref[...]

    qT = lax.dot_general(wqT_ref[...], u, _NT, preferred_element_type=F32)
    for hh in range(qT.shape[0] // HEAD_DIM):
        sl = slice(hh * HEAD_DIM, (hh + 1) * HEAD_DIM)
        qT_ref[0, sl, :] = head(qT[sl], gq, QSCALE).astype(BF16)

    kT = lax.dot_general(wkT_ref[...], u, _NT, preferred_element_type=F32)
    for pp in range(kT.shape[0] // 128):
        pair = jnp.concatenate(
            [head(kT[pp * 128 + e * 64: pp * 128 + (e + 1) * 64], gk, None) for e in range(2)], axis=0)
        k_ref[0, :, pp * 128:(pp + 1) * 128] = pair.T.astype(BF16)

    vT = lax.dot_general(wvT_ref[...], u, _NT, preferred_element_type=F32)
    vT_ref[0] = vT.astype(BF16)


def _inproj_t(h3d, g, wqT, wkT, wvT, ct, st, gq_b, gk_b, variant, tm=512):
    b, s, _ = h3d.shape
    nq, nk, nv = wqT.shape[0], wkT.shape[0], wvT.shape[0]
    kern = functools.partial(_inproj_t_kernel, variant=variant)
    return pl.pallas_call(
        kern,
        grid=(b, s // tm),
        in_specs=[
            pl.BlockSpec((1, tm, D_MODEL), lambda bi, ti: (bi, ti, 0)),
            _resident((1, D_MODEL), lambda bi, ti: (0, 0)),
            _resident((nq, D_MODEL), lambda bi, ti: (0, 0)),
            _resident((nk, D_MODEL), lambda bi, ti: (0, 0)),
            _resident((nv, D_MODEL), lambda bi, ti: (0, 0)),
            pl.BlockSpec((HEAD_DIM, tm), lambda bi, ti: (0, ti)),
            pl.BlockSpec((HEAD_DIM, tm), lambda bi, ti: (0, ti)),
            _resident((HEAD_DIM, tm), lambda bi, ti: (0, 0)),
            _resident((HEAD_DIM, tm), lambda bi, ti: (0, 0)),
        ],
        out_specs=[
            pl.BlockSpec((1, nq, tm), lambda bi, ti: (bi, 0, ti)),
            pl.BlockSpec((1, tm, nk), lambda bi, ti: (bi, ti, 0)),
            pl.BlockSpec((1, nv, tm), lambda bi, ti: (bi, 0, ti)),
        ],
        out_shape=[
            jax.ShapeDtypeStruct((b, nq, s), BF16),
            jax.ShapeDtypeStruct((b, s, nk), BF16),
            jax.ShapeDtypeStruct((b, nv, s), BF16),
        ],
        compiler_params=_cparams(2),
        name="inproj_" + variant,
    )(h3d, g, wqT, wkT, wvT, ct, st, gq_b, gk_b)


def _inproj_n_kernel(h_ref, g_ref, w_ref, q_ref, k_ref, v_ref):
    u = _rms_rows(h_ref[0], g_ref[...]).astype(BF16)
    y = jnp.dot(u, w_ref[...], preferred_element_type=F32)
    q_ref[0] = (y[:, 0:D_MODEL] * QSCALE).astype(BF16)
    k_ref[0] = y[:, D_MODEL:2 * D_MODEL].astype(BF16)
    v_ref[0] = y[:, 2 * D_MODEL:3 * D_MODEL].astype(BF16)


def _inproj_n(h3d, g, w, tm=512):
    b, s, _ = h3d.shape
    blk = pl.BlockSpec((1, tm, D_MODEL), lambda bi, ti: (bi, ti, 0))
    return pl.pallas_call(
        _inproj_n_kernel,
        grid=(b, s // tm),
        in_specs=[
            blk,
            _resident((1, D_MODEL), lambda bi, ti: (0, 0)),
            _resident((D_MODEL, 3 * D_MODEL), lambda bi, ti: (0, 0)),
        ],
        out_specs=[blk, blk, blk],
        out_shape=[jax.ShapeDtypeStruct((b, s, D_MODEL), BF16)] * 3,
        compiler_params=_cparams(2),
        name="inproj_na",
    )(h3d, g, w)


_SUM_ROWS = 16
_Q_UNROLL = 2
_MIN_PV_ROWS = 128
_RING = 4
_LOOKAHEAD = 3
_PV_LAG = 1
_P_RING = 2
_FLASH_LANES = {"diff": 1024, "gqa": 1024}


def _flash_kernel(qT_ref, k_ref, vT_ref, *rest, variant, qb, nq, kb, nk, unroll, lam_init):
    if variant == "diff":
        lam_ref, gsub_ref, o_ref, s_ring, p_ring, acc_sc = rest
    else:
        o_ref, s_ring, p_ring, acc_sc = rest

    def padded_queries(qq):
        qT = qT_ref[0, :, pl.ds(pl.multiple_of(qq * qb, qb), qb)]
        if variant == "diff":
            zero = jnp.zeros((HEAD_DIM, qb), BF16)
            return jnp.concatenate(
                [jnp.concatenate([qT[0:64], zero], axis=0),
                 jnp.concatenate([zero, qT[64:128]], axis=0)], axis=1)
        parity = pl.program_id(1) % 2
        row_half = (lax.broadcasted_iota(jnp.int32, (128, qb), 0) >= HEAD_DIM).astype(jnp.int32)
        keep = row_half == parity
        parts = []
        for g in range(4):
            qg = qT[g * 64:(g + 1) * 64]
            parts.append(jnp.where(keep, jnp.concatenate([qg, qg], axis=0), jnp.zeros((), BF16)))
        return jnp.concatenate(parts, axis=1)

    lanes = s_ring.shape[2]
    dv = vT_ref.shape[1]
    ones_rows = jnp.ones((max(_SUM_ROWS, _MIN_PV_ROWS - dv), kb), BF16)

    def scores(qz, i):
        ks = i * kb if isinstance(i, int) else pl.multiple_of(i * kb, kb)
        return jnp.dot(k_ref[0, pl.ds(ks, kb), :], qz, preferred_element_type=F32)

    def weighted_values(i, p):
        ks = pl.multiple_of(i * kb, kb)
        v_ext = jnp.concatenate([vT_ref[0, :, pl.ds(ks, kb)], ones_rows], axis=0)
        return jnp.dot(v_ext, p, preferred_element_type=F32)[0:dv + _SUM_ROWS]

    def first_scores(qq):
        qz = padded_queries(qq)
        first = [scores(qz, t) for t in range(_LOOKAHEAD)]
        for t in range(_LOOKAHEAD):
            s_ring[t] = first[t]
        return tuple(jnp.max(f, axis=0, keepdims=True) for f in first)

    def stage(qz, i, slot, stats, score_qz=None, score_block=None):
        alphas, m_prev, maxes = stats
        m_new = jnp.maximum(m_prev, maxes[0])
        s_new = scores(qz if score_qz is None else score_qz,
                       i + _LOOKAHEAD if score_block is None else score_block)
        s_ring[(slot + _LOOKAHEAD) % _RING] = s_new
        maxes = maxes[1:] + (jnp.max(s_new, axis=0, keepdims=True),)
        acc_sc[...] = alphas[0] * acc_sc[...] + weighted_values(
            jnp.maximum(i - _PV_LAG, 0), p_ring[(slot - _PV_LAG) % _P_RING])
        p_ring[slot % _P_RING] = jnp.exp2(s_ring[slot % _RING] - m_new).astype(BF16)
        return alphas[1:] + (jnp.exp2(m_prev - m_new),), m_new, maxes

    n_body = nk // unroll

    def query_block(qq, maxes):
        qz = padded_queries(qq)
        for t in range(1, _PV_LAG + 1):
            p_ring[-t % _P_RING] = jnp.zeros(p_ring.shape[1:], BF16)
        acc_sc[...] = jnp.zeros(acc_sc.shape, F32)
        stats = ((jnp.ones((1, lanes), F32),) * _PV_LAG, jnp.full((1, lanes), NEG, F32), maxes)

        def body(ii, stats):
            for u in range(unroll):
                stats = stage(qz, unroll * ii + u, u, stats)
            return stats

        stats = lax.fori_loop(0, n_body - 1, body, stats)
        qz_next = padded_queries(jnp.minimum(qq + 1, nq - 1))
        for u in range(unroll):
            i_last = unroll * (n_body - 1) + u
            ahead = i_last + _LOOKAHEAD
            if ahead < nk:
                stats = stage(qz, i_last, u, stats)
            else:
                stats = stage(qz, i_last, u, stats, score_qz=qz_next, score_block=ahead - nk)
        acc = acc_sc[...]
        for t in range(_PV_LAG):
            blk = nk - _PV_LAG + t
            acc = stats[0][t] * acc + weighted_values(blk, p_ring[blk % _P_RING])

        o = acc[0:dv] / acc[dv:dv + 1]
        rows = pl.ds(pl.multiple_of(qq * qb, qb), qb)
        if variant == "diff":
            lv = lam_ref[...]
            lam = (jnp.exp(jnp.sum(lv[0:1] * lv[1:2], axis=1, keepdims=True))
                   - jnp.exp(jnp.sum(lv[2:3] * lv[3:4], axis=1, keepdims=True)) + lam_init)
            od = o[:, 0:qb] - lam * o[:, qb:2 * qb]
            ms = jnp.mean(od * od, axis=0, keepdims=True)
            od = od * lax.rsqrt(ms + EPS) * gsub_ref[...] * (1.0 - lam_init)
            o_ref[0, rows, :] = od.T.astype(BF16)
        else:
            stacked = jnp.concatenate([o[:, g * qb:(g + 1) * qb] for g in range(4)], axis=0)
            o_ref[0, rows, :] = stacked.T.astype(BF16)
        return stats[2]

    assert nk % _RING == 0 and _LOOKAHEAD <= unroll and nq % _Q_UNROLL == 0

    def query_blocks(qg, maxes):
        for t in range(_Q_UNROLL):
            maxes = query_block(_Q_UNROLL * qg + t, maxes)
        return maxes

    lax.fori_loop(0, nq // _Q_UNROLL, query_blocks, first_scores(0))


def _flash(qT, k, vT, variant, lam_vecs=None, gsub_b=None, lam_init=0.0, kb=256, unroll=8):
    b, _, s = qT.shape
    lanes = _FLASH_LANES[variant]
    if variant == "diff":
        qb, n_groups, r, dv, width = lanes // 2, 8, 128, 128, 128
        k_map = lambda bi, j: (bi, 0, j)
    else:
        qb, n_groups, r, dv, width = lanes // 4, 4, 256, 64, 256
        k_map = lambda bi, j: (bi, 0, j // 2)
    nk = s // kb
    assert unroll % _RING == 0 and nk % unroll == 0
    kern = functools.partial(_flash_kernel, variant=variant, qb=qb, nq=s // qb, kb=kb, nk=nk,
                             unroll=unroll, lam_init=lam_init)
    in_specs = [
        pl.BlockSpec((1, r, s), lambda bi, j: (bi, j, 0)),
        pl.BlockSpec((1, s, 128), k_map),
        pl.BlockSpec((1, dv, s), lambda bi, j: (bi, j, 0)),
    ]
    args = [qT, k, vT]
    if variant == "diff":
        in_specs += [
            _resident((4, HEAD_DIM), lambda bi, j: (0, 0)),
            _resident((128, qb), lambda bi, j: (0, 0)),
        ]
        args += [lam_vecs, gsub_b]
    return pl.pallas_call(
        kern,
        grid=(b, n_groups),
        in_specs=in_specs,
        out_specs=pl.BlockSpec((1, s, width), lambda bi, j: (bi, 0, j)),
        out_shape=jax.ShapeDtypeStruct((b, s, D_MODEL), BF16),
        scratch_shapes=[
            pltpu.VMEM((_RING, kb, lanes), F32),
            pltpu.VMEM((_P_RING, kb, lanes), BF16),
            pltpu.VMEM((dv + _SUM_ROWS, lanes), F32),
        ],
        compiler_params=_cparams(2),
        name="flash_" + variant,
    )(*args)


def _na_table_kernel(rpb_ref, pt_ref):
    cq = lax.broadcasted_iota(jnp.int32, (GRID_W, 128), 0)
    ck = lax.broadcasted_iota(jnp.int32, (GRID_W, 128), 1) & (GRID_W - 1)
    c0 = jnp.clip(cq - NA_WIN_W // 2, 0, GRID_W - NA_WIN_W)
    valid = (ck >= c0) & (ck < c0 + NA_WIN_W)
    for dr in range(2 * NA_WIN_H - 2):
        a = rpb_ref[0, dr:dr + 1, :]
        bb = rpb_ref[0, dr + 1:dr + 2, :]
        x = a + pltpu.roll(bb, GRID_W, axis=1)
        xb = jnp.broadcast_to(x, (GRID_W, 128))
        t = pltpu.roll(xb, 128 - (NA_WIN_W - 1), axis=1, stride=1, stride_axis=0)
        pt_ref[0, dr] = jnp.where(valid, t * LOG2E, NEG)


def _na_table(rpb_pad):
    nh = rpb_pad.shape[0]
    ndr = 2 * NA_WIN_H - 2
    return pl.pallas_call(
        _na_table_kernel,
        grid=(nh,),
        in_specs=[pl.BlockSpec((1, 16, 128), lambda h: (h, 0, 0))],
        out_specs=pl.BlockSpec((1, ndr, GRID_W, 128), lambda h: (h, 0, 0, 0)),
        out_shape=jax.ShapeDtypeStruct((nh, ndr, GRID_W, 128), F32),
        compiler_params=_cparams(1),
        name="na_table",
    )(rpb_pad)


_NA_ROWS = 4
_NA_BAND = _NA_ROWS + NA_WIN_H


def _na_kernel(q_ref, k0_ref, k1_ref, k2_ref, v0_ref, v1_ref, v2_ref, pt_ref, o_ref, kband, vband,
               *, n_steps):
    i = pl.program_id(1)
    blk = _NA_ROWS * GRID_W
    for n, (kr, vr) in enumerate(((k0_ref, v0_ref), (k1_ref, v1_ref), (k2_ref, v2_ref))):
        kband[n * blk:(n + 1) * blk, :] = kr[0]
        vband[n * blk:(n + 1) * blk, :] = vr[0]
    left = lax.broadcasted_iota(jnp.int32, (GRID_W, 128), 1) < HEAD_DIM
    nkeys = NA_WIN_H * GRID_W

    def row_body(t, carry):
        off = jnp.where(i == 0, 0, jnp.where(i == n_steps - 1, _NA_ROWS, t))
        sidx = jnp.where(i == 0, NA_WIN_H - 1 - t, jnp.where(i == n_steps - 1, 3 - t, 3))
        start = pl.multiple_of(off * GRID_W, GRID_W)
        qs = pl.multiple_of(t * GRID_W, GRID_W)
        n_pairs = D_MODEL // 128
        cols = [slice(j * 128, (j + 1) * 128) for j in range(n_pairs)]
        scores = []
        for j in range(n_pairs):
            qp = q_ref[0, pl.ds(qs, GRID_W), cols[j]]
            kp = kband[pl.ds(start, nkeys), cols[j]]
            for e in range(2):
                qz = jnp.where(left if e == 0 else jnp.logical_not(left), qp, jnp.zeros((), BF16))
                scores.append(lax.dot_general(qz, kp, _NT, preferred_element_type=F32))
        probs, denoms = [], []
        for h in range(2 * n_pairs):
            bias = jnp.concatenate([pt_ref[h, sidx + 2 * m] for m in range(NA_WIN_H // 2)], axis=1)
            s = scores[h] + bias
            p = jnp.exp2(s - jnp.max(s, axis=1, keepdims=True))
            denoms.append(jnp.sum(p, axis=1, keepdims=True))
            probs.append(p.astype(BF16))
        for j in range(n_pairs):
            vp = vband[pl.ds(start, nkeys), cols[j]]
            outs = [jnp.dot(probs[2 * j + e], vp, preferred_element_type=F32) / denoms[2 * j + e]
                    for e in range(2)]
            o_ref[0, pl.ds(qs, GRID_W), cols[j]] = jnp.where(left, outs[0], outs[1]).astype(BF16)
        return carry

    lax.fori_loop(0, _NA_ROWS, row_body, 0)


def _na_attention(q, k, v, pt):
    b, s, _ = q.shape
    blk = _NA_ROWS * GRID_W
    n_steps = s // blk
    last = n_steps - 3

    def band_spec(n):
        return pl.BlockSpec((1, blk, D_MODEL),
                            lambda bi, i: (bi, jnp.clip(i - 1, 0, last) + n, 0))

    kern = functools.partial(_na_kernel, n_steps=n_steps)
    return pl.pallas_call(
        kern,
        grid=(b, n_steps),
        in_specs=[
            pl.BlockSpec((1, blk, D_MODEL), lambda bi, i: (bi, i, 0)),
            band_spec(0), band_spec(1), band_spec(2),
            band_spec(0), band_spec(1), band_spec(2),
            _resident(pt.shape, lambda bi, i: (0, 0, 0, 0)),
        ],
        out_specs=pl.BlockSpec((1, blk, D_MODEL), lambda bi, i: (bi, i, 0)),
        out_shape=jax.ShapeDtypeStruct((b, s, D_MODEL), BF16),
        scratch_shapes=[
            pltpu.VMEM((_NA_BAND * GRID_W, D_MODEL), BF16),
            pltpu.VMEM((_NA_BAND * GRID_W, D_MODEL), BF16),
        ],
        compiler_params=_cparams(2),
        name="na_attention",
    )(q, k, k, k, v, v, v, pt)


def _angles(pos, dims, theta):
    inv = theta ** (-jnp.arange(0, dims, 2, dtype=F32) / dims)
    return pos[:, None] * inv[None, :]


def _diff_tables(s):
    ang = _angles(jnp.arange(s, dtype=F32), ROPE_DIMS, ROPE_THETA)
    c, sn = jnp.cos(ang).T, jnp.sin(ang).T
    ones = jnp.ones((HEAD_DIM - ROPE_DIMS, s), F32)
    ct = jnp.concatenate([c, c, ones], axis=0)
    st = jnp.concatenate([-sn, sn, 0.0 * ones], axis=0)
    return ct, st


def _axial_tables(s):
    t = jnp.arange(s)
    half = HEAD_DIM // 2
    ar = _angles((t // GRID_W).astype(F32), half, AXIAL_THETA)
    ac = _angles((t % GRID_W).astype(F32), half, AXIAL_THETA)
    cr, sr, cc, sc = jnp.cos(ar).T, jnp.sin(ar).T, jnp.cos(ac).T, jnp.sin(ac).T
    ct = jnp.concatenate([cr, cr, cc, cc], axis=0)
    st = jnp.concatenate([-sr, sr, -sc, sc], axis=0)
    return ct, st


def _col_gain(g, tm):
    return jnp.broadcast_to(g.astype(F32)[:, None], (g.shape[0], tm))


def kernel(x, norm_g, ffn1_wg, ffn1_wu, ffn1_wd, ffn2_wg, ffn2_wu, ffn2_wd,
           diff_w_in, diff_w_out, diff_lambda, diff_subln,
           na_w_in, na_w_out, na_rpb,
           gqa_w_in, gqa_w_out, gqa_qk_norm):
    b, s, d = x.shape
    n = b * s
    tm = 512
    h = x.reshape(n, d)
    ones_gain = jnp.ones((HEAD_DIM, tm), F32)

    for i in range(DEPTH):
        g = norm_g[i]
        h = _ffn(h, g[0:2], ffn1_wg[i].astype(BF16), ffn1_wu[i].astype(BF16), ffn1_wd[i].astype(BF16))
        kind, j = i % N_MIXERS, i // N_MIXERS
        h3 = h.reshape(b, s, d)
        if kind == 0:
            w = diff_w_in[j]
            wqT = w[:, 0:d].T.astype(BF16)
            wkT = w[:, d:2 * d].T.astype(BF16)
            wvT = w[:, 2 * d:3 * d].T.astype(BF16)
            ct, st = _diff_tables(s)
            qT, k, vT = _inproj_t(h3, g[2:3], wqT, wkT, wvT, ct, st, ones_gain, ones_gain, "diff", tm)
            lam_init = 0.8 - 0.6 * math.exp(-0.3 * i)
            gsub_b = _col_gain(diff_subln[j], _FLASH_LANES["diff"] // 2)
            o = _flash(qT, k, vT, "diff", diff_lambda[j].astype(F32), gsub_b, lam_init, kb=512)
            w_out = diff_w_out[j]
        elif kind == 1:
            q, k, v = _inproj_n(h3, g[2:3], na_w_in[j].astype(BF16), tm)
            rpb = na_rpb[j].astype(F32)
            rpb_pad = jnp.zeros((rpb.shape[0], 16, 128), F32).at[:, :rpb.shape[1], :rpb.shape[2]].set(rpb)
            pt = _na_table(rpb_pad)
            o = _na_attention(q, k, v, pt)
            w_out = na_w_out[j]
        else:
            w = gqa_w_in[j]
            nq = d
            nkv = (w.shape[1] - nq) // 2
            wqT = w[:, 0:nq].T.astype(BF16)
            wkT = w[:, nq:nq + nkv].T.astype(BF16)
            wvT = w[:, nq + nkv:].T.astype(BF16)
            ct, st = _axial_tables(s)
            gq_b = _col_gain(gqa_qk_norm[j, 0], tm)
            gk_b = _col_gain(gqa_qk_norm[j, 1], tm)
            qT, k, vT = _inproj_t(h3, g[2:3], wqT, wkT, wvT, ct, st, gq_b, gk_b, "gqa", tm)
            o = _flash(qT, k, vT, "gqa", kb=512)
            w_out = gqa_w_out[j]
        h = _outproj_ffn(o.reshape(n, d), h, g[3:6], w_out.astype(BF16),
                         ffn2_wg[i].astype(BF16), ffn2_wu[i].astype(BF16), ffn2_wd[i].astype(BF16), tm)
    return h.reshape(b, s, d)
```

```python
import functools
import math

import jax
import jax.numpy as jnp
from jax import lax
from jax.experimental import pallas as pl
from jax.experimental.pallas import tpu as pltpu

D_MODEL = 1024
SEQ = 8192
DEPTH = 4
N_MIXERS = 3
GRID_W = 64
EPS = 1e-6
D_FF = 2816
FFN_RES = 0.5
HEAD_DIM = 64
ROPE_THETA = 500000.0
ROPE_DIMS = 16
NA_WIN_H = 8
NA_WIN_W = 16
AXIAL_THETA = 10000.0

LOG2E = 1.4426950408889634
QSCALE = (HEAD_DIM ** -0.5) * LOG2E
NEG = -1e30

VMEM_LIMIT_BYTES = 56 * 1024 * 1024
TOKEN_TILE = 512
FLASH_KEY_BLOCK = 512
FLASH_UNROLL = 8

BF16 = jnp.bfloat16
F32 = jnp.float32

_NT = (((1,), (1,)), ((), ()))


def _cparams(n_axes):
    return pltpu.CompilerParams(
        dimension_semantics=("parallel",) * n_axes,
        vmem_limit_bytes=VMEM_LIMIT_BYTES,
    )


def _resident(shape, index_map):
    return pl.BlockSpec(shape, index_map, pipeline_mode=pl.Buffered(1))


def _rms_rows(x, g):
    ms = jnp.mean(x * x, axis=-1, keepdims=True)
    return x * lax.rsqrt(ms + EPS) * g


def _ffn_block(x, g_pre, g_post, wg_ref, wu_ref, wd_ref):
    xn = _rms_rows(x, g_pre).astype(BF16)
    gate = jnp.dot(xn, wg_ref[...], preferred_element_type=F32)
    up = jnp.dot(xn, wu_ref[...], preferred_element_type=F32)
    act = (gate * jax.nn.sigmoid(gate) * up).astype(BF16)
    y = jnp.dot(act, wd_ref[...], preferred_element_type=F32)
    return x + FFN_RES * _rms_rows(y, g_post)


def _ffn_kernel(x_ref, g_ref, wg_ref, wu_ref, wd_ref, o_ref):
    g = g_ref[...]
    o_ref[...] = _ffn_block(x_ref[...], g[0:1], g[1:2], wg_ref, wu_ref, wd_ref)


def _outproj_ffn_kernel(o_ref, h_ref, g_ref, wo_ref, wg_ref, wu_ref, wd_ref, out_ref):
    g = g_ref[...]
    m = jnp.dot(o_ref[...], wo_ref[...], preferred_element_type=F32)
    h1 = h_ref[...] + _rms_rows(m, g[0:1])
    out_ref[...] = _ffn_block(h1, g[1:2], g[2:3], wg_ref, wu_ref, wd_ref)


def _outproj_ffn(o2d, h2d, g3, wo, wg, wu, wd, tm=TOKEN_TILE):
    n = h2d.shape[0]
    blk = pl.BlockSpec((tm, D_MODEL), lambda i: (i, 0))
    return pl.pallas_call(
        _outproj_ffn_kernel,
        grid=(n // tm,),
        in_specs=[
            blk, blk,
            _resident((3, D_MODEL), lambda i: (0, 0)),
            _resident((D_MODEL, D_MODEL), lambda i: (0, 0)),
            _resident((D_MODEL, D_FF), lambda i: (0, 0)),
            _resident((D_MODEL, D_FF), lambda i: (0, 0)),
            _resident((D_FF, D_MODEL), lambda i: (0, 0)),
        ],
        out_specs=blk,
        out_shape=jax.ShapeDtypeStruct(h2d.shape, F32),
        compiler_params=_cparams(1),
        name="outproj_ffn",
    )(o2d, h2d, g3, wo, wg, wu, wd)


def _ffn(h2d, g2, wg, wu, wd, tm=TOKEN_TILE):
    n = h2d.shape[0]
    return pl.pallas_call(
        _ffn_kernel,
        grid=(n // tm,),
        in_specs=[
            pl.BlockSpec((tm, D_MODEL), lambda i: (i, 0)),
            _resident((2, D_MODEL), lambda i: (0, 0)),
            _resident((D_MODEL, D_FF), lambda i: (0, 0)),
            _resident((D_MODEL, D_FF), lambda i: (0, 0)),
            _resident((D_FF, D_MODEL), lambda i: (0, 0)),
        ],
        out_specs=pl.BlockSpec((tm, D_MODEL), lambda i: (i, 0)),
        out_shape=jax.ShapeDtypeStruct(h2d.shape, F32),
        compiler_params=_cparams(1),
        name="ffn",
    )(h2d, g2, wg, wu, wd)


def _swap_halves(x, half):
    blocks = []
    r = 0
    n = x.shape[0]
    while r + 2 * half <= n:
        blocks += [x[r + half:r + 2 * half], x[r:r + half]]
        r += 2 * half
    return jnp.concatenate(blocks, axis=0)


def _inproj_t_kernel(h_ref, g_ref, wqT_ref, wkT_ref, wvT_ref, ct_ref, st_ref, *rest, variant):
    if variant == "gqa":
        gq_ref, gk_ref, qT_ref, k_ref, vT_ref = rest
        gq, gk = gq_ref[...], gk_ref[...]
    else:
        qT_ref, k_ref, vT_ref = rest
        gq = gk = None
    u = _rms_rows(h_ref[0], g_ref[...]).astype(BF16)
    ct = ct_ref[...]
    st = st_ref[...]

    def head(x, gain, scale):
        if variant == "gqa":
            ms = jnp.mean(x * x, axis=0, keepdims=True)
            x = x * lax.rsqrt(ms + EPS) * gain
            xs = _swap_halves(x, 16)
        else:
            xs = jnp.concatenate([x[8:16], x[0:8], x[16:64]], axis=0)
        x = x * ct + xs * st
        if scale is not None:
            x = x * scale
        return x

    qT = lax.dot_general(wqT_ref[...], u, _NT, preferred_element_type=F32)
    for hh in range(qT.shape[0] // HEAD_DIM):
        sl = slice(hh * HEAD_DIM, (hh + 1) * HEAD_DIM)
        qT_ref[0, sl, :] = head(qT[sl], gq, QSCALE).astype(BF16)

    kT = lax.dot_general(wkT_ref[...], u, _NT, preferred_element_type=F32)
    for pp in range(kT.shape[0] // 128):
        pair = jnp.concatenate(
            [head(kT[pp * 128 + e * 64: pp * 128 + (e + 1) * 64], gk, None) for e in range(2)], axis=0)
        k_ref[0, :, pp * 128:(pp + 1) * 128] = pair.T.astype(BF16)

    vT = lax.dot_general(wvT_ref[...], u, _NT, preferred_element_type=F32)
    vT_ref[0] = vT.astype(BF16)


def _inproj_t(h3d, g, wqT, wkT, wvT, ct, st, variant, qk_gains=(), tm=TOKEN_TILE):
    b, s, _ = h3d.shape
    nq, nk, nv = wqT.shape[0], wkT.shape[0], wvT.shape[0]
    assert len(qk_gains) == (2 if variant == "gqa" else 0)
    kern = functools.partial(_inproj_t_kernel, variant=variant)
    return pl.pallas_call(
        kern,
        grid=(b, s // tm),
        in_specs=[
            pl.BlockSpec((1, tm, D_MODEL), lambda bi, ti: (bi, ti, 0)),
            _resident((1, D_MODEL), lambda bi, ti: (0, 0)),
            _resident((nq, D_MODEL), lambda bi, ti: (0, 0)),
            _resident((nk, D_MODEL), lambda bi, ti: (0, 0)),
            _resident((nv, D_MODEL), lambda bi, ti: (0, 0)),
            pl.BlockSpec((HEAD_DIM, tm), lambda bi, ti: (0, ti)),
            pl.BlockSpec((HEAD_DIM, tm), lambda bi, ti: (0, ti)),
        ] + [_resident((HEAD_DIM, tm), lambda bi, ti: (0, 0)) for _ in qk_gains],
        out_specs=[
            pl.BlockSpec((1, nq, tm), lambda bi, ti: (bi, 0, ti)),
            pl.BlockSpec((1, tm, nk), lambda bi, ti: (bi, ti, 0)),
            pl.BlockSpec((1, nv, tm), lambda bi, ti: (bi, 0, ti)),
        ],
        out_shape=[
            jax.ShapeDtypeStruct((b, nq, s), BF16),
            jax.ShapeDtypeStruct((b, s, nk), BF16),
            jax.ShapeDtypeStruct((b, nv, s), BF16),
        ],
        compiler_params=_cparams(2),
        name="inproj_" + variant,
    )(h3d, g, wqT, wkT, wvT, ct, st, *qk_gains)


def _inproj_n_kernel(h_ref, g_ref, w_ref, q_ref, k_ref, v_ref):
    u = _rms_rows(h_ref[0], g_ref[...]).astype(BF16)
    y = jnp.dot(u, w_ref[...], preferred_element_type=F32)
    q_ref[0] = (y[:, 0:D_MODEL] * QSCALE).astype(BF16)
    k_ref[0] = y[:, D_MODEL:2 * D_MODEL].astype(BF16)
    v_ref[0] = y[:, 2 * D_MODEL:3 * D_MODEL].astype(BF16)


def _inproj_n(h3d, g, w, tm=TOKEN_TILE):
    b, s, _ = h3d.shape
    blk = pl.BlockSpec((1, tm, D_MODEL), lambda bi, ti: (bi, ti, 0))
    return pl.pallas_call(
        _inproj_n_kernel,
        grid=(b, s // tm),
        in_specs=[
            blk,
            _resident((1, D_MODEL), lambda bi, ti: (0, 0)),
            _resident((D_MODEL, 3 * D_MODEL), lambda bi, ti: (0, 0)),
        ],
        out_specs=[blk, blk, blk],
        out_shape=[jax.ShapeDtypeStruct((b, s, D_MODEL), BF16)] * 3,
        compiler_params=_cparams(2),
        name="inproj_na",
    )(h3d, g, w)


_SUM_ROWS = 16
_Q_UNROLL = 2
_MIN_PV_ROWS = 128
_RING = 4
_LOOKAHEAD = 3
_PV_LAG = 1
_P_RING = 2
_FLASH_LANES = {"diff": 1024, "gqa": 1024}


def _flash_kernel(qT_ref, k_ref, vT_ref, *rest, variant, qb, nq, kb, nk, unroll, lam_init):
    if variant == "diff":
        lam_ref, gsub_ref, o_ref, s_ring, p_ring, acc_sc = rest
    else:
        o_ref, s_ring, p_ring, acc_sc = rest

    def padded_queries(qq):
        qT = qT_ref[0, :, pl.ds(pl.multiple_of(qq * qb, qb), qb)]
        if variant == "diff":
            zero = jnp.zeros((HEAD_DIM, qb), BF16)
            return jnp.concatenate(
                [jnp.concatenate([qT[0:64], zero], axis=0),
                 jnp.concatenate([zero, qT[64:128]], axis=0)], axis=1)
        parity = pl.program_id(1) % 2
        row_half = (lax.broadcasted_iota(jnp.int32, (128, qb), 0) >= HEAD_DIM).astype(jnp.int32)
        keep = row_half == parity
        parts = []
        for g in range(4):
            qg = qT[g * 64:(g + 1) * 64]
            parts.append(jnp.where(keep, jnp.concatenate([qg, qg], axis=0), jnp.zeros((), BF16)))
        return jnp.concatenate(parts, axis=1)

    lanes = s_ring.shape[2]
    dv = vT_ref.shape[1]
    ones_rows = jnp.ones((max(_SUM_ROWS, _MIN_PV_ROWS - dv), kb), BF16)

    def scores(qz, i):
        ks = i * kb if isinstance(i, int) else pl.multiple_of(i * kb, kb)
        return jnp.dot(k_ref[0, pl.ds(ks, kb), :], qz, preferred_element_type=F32)

    def weighted_values(i, p):
        ks = pl.multiple_of(i * kb, kb)
        v_ext = jnp.concatenate([vT_ref[0, :, pl.ds(ks, kb)], ones_rows], axis=0)
        return jnp.dot(v_ext, p, preferred_element_type=F32)[0:dv + _SUM_ROWS]

    def first_scores(qq):
        qz = padded_queries(qq)
        first = [scores(qz, t) for t in range(_LOOKAHEAD)]
        for t in range(_LOOKAHEAD):
            s_ring[t] = first[t]
        return tuple(jnp.max(f, axis=0, keepdims=True) for f in first)

    def stage(qz, i, slot, stats, score_qz=None, score_block=None):
        alphas, m_prev, maxes = stats
        m_new = jnp.maximum(m_prev, maxes[0])
        s_new = scores(qz if score_qz is None else score_qz,
                       i + _LOOKAHEAD if score_block is None else score_block)
        s_ring[(slot + _LOOKAHEAD) % _RING] = s_new
        maxes = maxes[1:] + (jnp.max(s_new, axis=0, keepdims=True),)
        acc_sc[...] = alphas[0] * acc_sc[...] + weighted_values(
            jnp.maximum(i - _PV_LAG, 0), p_ring[(slot - _PV_LAG) % _P_RING])
        p_ring[slot % _P_RING] = jnp.exp2(s_ring[slot % _RING] - m_new).astype(BF16)
        return alphas[1:] + (jnp.exp2(m_prev - m_new),), m_new, maxes

    n_body = nk // unroll

    def query_block(qq, maxes):
        qz = padded_queries(qq)
        for t in range(1, _PV_LAG + 1):
            p_ring[-t % _P_RING] = jnp.zeros(p_ring.shape[1:], BF16)
        acc_sc[...] = jnp.zeros(acc_sc.shape, F32)
        stats = ((jnp.ones((1, lanes), F32),) * _PV_LAG, jnp.full((1, lanes), NEG, F32), maxes)

        def body(ii, stats):
            for u in range(unroll):
                stats = stage(qz, unroll * ii + u, u, stats)
            return stats

        stats = lax.fori_loop(0, n_body - 1, body, stats)
        qz_next = padded_queries(jnp.minimum(qq + 1, nq - 1))
        for u in range(unroll):
            i_last = unroll * (n_body - 1) + u
            ahead = i_last + _LOOKAHEAD
            if ahead < nk:
                stats = stage(qz, i_last, u, stats)
            else:
                stats = stage(qz, i_last, u, stats, score_qz=qz_next, score_block=ahead - nk)
        acc = acc_sc[...]
        for t in range(_PV_LAG):
            blk = nk - _PV_LAG + t
            acc = stats[0][t] * acc + weighted_values(blk, p_ring[blk % _P_RING])

        o = acc[0:dv] / acc[dv:dv + 1]
        rows = pl.ds(pl.multiple_of(qq * qb, qb), qb)
        if variant == "diff":
            lv = lam_ref[...]
            lam = (jnp.exp(jnp.sum(lv[0:1] * lv[1:2], axis=1, keepdims=True))
                   - jnp.exp(jnp.sum(lv[2:3] * lv[3:4], axis=1, keepdims=True)) + lam_init)
            od = o[:, 0:qb] - lam * o[:, qb:2 * qb]
            ms = jnp.mean(od * od, axis=0, keepdims=True)
            od = od * lax.rsqrt(ms + EPS) * gsub_ref[...] * (1.0 - lam_init)
            o_ref[0, rows, :] = od.T.astype(BF16)
        else:
            stacked = jnp.concatenate([o[:, g * qb:(g + 1) * qb] for g in range(4)], axis=0)
            o_ref[0, rows, :] = stacked.T.astype(BF16)
        return stats[2]

    assert nk % _RING == 0 and _LOOKAHEAD <= unroll and nq % _Q_UNROLL == 0

    def query_blocks(qg, maxes):
        for t in range(_Q_UNROLL):
            maxes = query_block(_Q_UNROLL * qg + t, maxes)
        return maxes

    lax.fori_loop(0, nq // _Q_UNROLL, query_blocks, first_scores(0))


def _flash(qT, k, vT, variant, lam_vecs=None, gsub_b=None, lam_init=0.0,
           kb=FLASH_KEY_BLOCK, unroll=FLASH_UNROLL):
    b, _, s = qT.shape
    lanes = _FLASH_LANES[variant]
    if variant == "diff":
        qb, n_groups, r, dv, width = lanes // 2, 8, 128, 128, 128
        k_map = lambda bi, j: (bi, 0, j)
    else:
        qb, n_groups, r, dv, width = lanes // 4, 4, 256, 64, 256
        k_map = lambda bi, j: (bi, 0, j // 2)
    nk = s // kb
    assert unroll % _RING == 0 and nk % unroll == 0
    kern = functools.partial(_flash_kernel, variant=variant, qb=qb, nq=s // qb, kb=kb, nk=nk,
                             unroll=unroll, lam_init=lam_init)
    in_specs = [
        pl.BlockSpec((1, r, s), lambda bi, j: (bi, j, 0)),
        pl.BlockSpec((1, s, 128), k_map),
        pl.BlockSpec((1, dv, s), lambda bi, j: (bi, j, 0)),
    ]
    args = [qT, k, vT]
    if variant == "diff":
        in_specs += [
            _resident((4, HEAD_DIM), lambda bi, j: (0, 0)),
            _resident((128, qb), lambda bi, j: (0, 0)),
        ]
        args += [lam_vecs, gsub_b]
    return pl.pallas_call(
        kern,
        grid=(b, n_groups),
        in_specs=in_specs,
        out_specs=pl.BlockSpec((1, s, width), lambda bi, j: (bi, 0, j)),
        out_shape=jax.ShapeDtypeStruct((b, s, D_MODEL), BF16),
        scratch_shapes=[
            pltpu.VMEM((_RING, kb, lanes), F32),
            pltpu.VMEM((_P_RING, kb, lanes), BF16),
            pltpu.VMEM((dv + _SUM_ROWS, lanes), F32),
        ],
        compiler_params=_cparams(2),
        name="flash_" + variant,
    )(*args)


def _na_table_kernel(rpb_ref, pt_ref):
    cq = lax.broadcasted_iota(jnp.int32, (GRID_W, 128), 0)
    ck = lax.broadcasted_iota(jnp.int32, (GRID_W, 128), 1) & (GRID_W - 1)
    c0 = jnp.clip(cq - NA_WIN_W // 2, 0, GRID_W - NA_WIN_W)
    valid = (ck >= c0) & (ck < c0 + NA_WIN_W)
    for dr in range(2 * NA_WIN_H - 2):
        a = rpb_ref[0, dr:dr + 1, :]
        bb = rpb_ref[0, dr + 1:dr + 2, :]
        x = a + pltpu.roll(bb, GRID_W, axis=1)
        xb = jnp.broadcast_to(x, (GRID_W, 128))
        t = pltpu.roll(xb, 128 - (NA_WIN_W - 1), axis=1, stride=1, stride_axis=0)
        pt_ref[0, dr] = jnp.where(valid, t * LOG2E, NEG)


def _na_table(rpb_pad):
    nh = rpb_pad.shape[0]
    ndr = 2 * NA_WIN_H - 2
    return pl.pallas_call(
        _na_table_kernel,
        grid=(nh,),
        in_specs=[pl.BlockSpec((1, 16, 128), lambda h: (h, 0, 0))],
        out_specs=pl.BlockSpec((1, ndr, GRID_W, 128), lambda h: (h, 0, 0, 0)),
        out_shape=jax.ShapeDtypeStruct((nh, ndr, GRID_W, 128), F32),
        compiler_params=_cparams(1),
        name="na_table",
    )(rpb_pad)


_NA_ROWS = 4
_NA_BAND = _NA_ROWS + NA_WIN_H


def _na_kernel(q_ref, k0_ref, k1_ref, k2_ref, v0_ref, v1_ref, v2_ref, pt_ref, o_ref, kband, vband,
               *, n_steps):
    i = pl.program_id(1)
    blk = _NA_ROWS * GRID_W
    for n, (kr, vr) in enumerate(((k0_ref, v0_ref), (k1_ref, v1_ref), (k2_ref, v2_ref))):
        kband[n * blk:(n + 1) * blk, :] = kr[0]
        vband[n * blk:(n + 1) * blk, :] = vr[0]
    left = lax.broadcasted_iota(jnp.int32, (GRID_W, 128), 1) < HEAD_DIM
    nkeys = NA_WIN_H * GRID_W

    def row_body(t, carry):
        off = jnp.where(i == 0, 0, jnp.where(i == n_steps - 1, _NA_ROWS, t))
        sidx = jnp.where(i == 0, NA_WIN_H - 1 - t, jnp.where(i == n_steps - 1, 3 - t, 3))
        start = pl.multiple_of(off * GRID_W, GRID_W)
        qs = pl.multiple_of(t * GRID_W, GRID_W)
        n_pairs = D_MODEL // 128
        cols = [slice(j * 128, (j + 1) * 128) for j in range(n_pairs)]
        scores = []
        for j in range(n_pairs):
            qp = q_ref[0, pl.ds(qs, GRID_W), cols[j]]
            kp = kband[pl.ds(start, nkeys), cols[j]]
            for e in range(2):
                qz = jnp.where(left if e == 0 else jnp.logical_not(left), qp, jnp.zeros((), BF16))
                scores.append(lax.dot_general(qz, kp, _NT, preferred_element_type=F32))
        probs, denoms = [], []
        for h in range(2 * n_pairs):
            bias = jnp.concatenate([pt_ref[h, sidx + 2 * m] for m in range(NA_WIN_H // 2)], axis=1)
            s = scores[h] + bias
            p = jnp.exp2(s - jnp.max(s, axis=1, keepdims=True))
            denoms.append(jnp.sum(p, axis=1, keepdims=True))
            probs.append(p.astype(BF16))
        for j in range(n_pairs):
            vp = vband[pl.ds(start, nkeys), cols[j]]
            outs = [jnp.dot(probs[2 * j + e], vp, preferred_element_type=F32) / denoms[2 * j + e]
                    for e in range(2)]
            o_ref[0, pl.ds(qs, GRID_W), cols[j]] = jnp.where(left, outs[0], outs[1]).astype(BF16)
        return carry

    lax.fori_loop(0, _NA_ROWS, row_body, 0)


def _na_attention(q, k, v, pt):
    b, s, _ = q.shape
    blk = _NA_ROWS * GRID_W
    n_steps = s // blk
    last = n_steps - 3

    def band_spec(n):
        return pl.BlockSpec((1, blk, D_MODEL),
                            lambda bi, i: (bi, jnp.clip(i - 1, 0, last) + n, 0))

    kern = functools.partial(_na_kernel, n_steps=n_steps)
    return pl.pallas_call(
        kern,
        grid=(b, n_steps),
        in_specs=[
            pl.BlockSpec((1, blk, D_MODEL), lambda bi, i: (bi, i, 0)),
            band_spec(0), band_spec(1), band_spec(2),
            band_spec(0), band_spec(1), band_spec(2),
            _resident(pt.shape, lambda bi, i: (0, 0, 0, 0)),
        ],
        out_specs=pl.BlockSpec((1, blk, D_MODEL), lambda bi, i: (bi, i, 0)),
        out_shape=jax.ShapeDtypeStruct((b, s, D_MODEL), BF16),
        scratch_shapes=[
            pltpu.VMEM((_NA_BAND * GRID_W, D_MODEL), BF16),
            pltpu.VMEM((_NA_BAND * GRID_W, D_MODEL), BF16),
        ],
        compiler_params=_cparams(2),
        name="na_attention",
    )(q, k, k, k, v, v, v, pt)


def _angles(pos, dims, theta):
    inv = theta ** (-jnp.arange(0, dims, 2, dtype=F32) / dims)
    return pos[:, None] * inv[None, :]


def _diff_tables(s):
    ang = _angles(jnp.arange(s, dtype=F32), ROPE_DIMS, ROPE_THETA)
    c, sn = jnp.cos(ang).T, jnp.sin(ang).T
    ones = jnp.ones((HEAD_DIM - ROPE_DIMS, s), F32)
    ct = jnp.concatenate([c, c, ones], axis=0)
    st = jnp.concatenate([-sn, sn, 0.0 * ones], axis=0)
    return ct, st


def _axial_tables(s):
    t = jnp.arange(s)
    half = HEAD_DIM // 2
    ar = _angles((t // GRID_W).astype(F32), half, AXIAL_THETA)
    ac = _angles((t % GRID_W).astype(F32), half, AXIAL_THETA)
    cr, sr, cc, sc = jnp.cos(ar).T, jnp.sin(ar).T, jnp.cos(ac).T, jnp.sin(ac).T
    ct = jnp.concatenate([cr, cr, cc, cc], axis=0)
    st = jnp.concatenate([-sr, sr, -sc, sc], axis=0)
    return ct, st


def _col_gain(g, tm):
    return jnp.broadcast_to(g.astype(F32)[:, None], (g.shape[0], tm))


def kernel(x, norm_g, ffn1_wg, ffn1_wu, ffn1_wd, ffn2_wg, ffn2_wu, ffn2_wd,
           diff_w_in, diff_w_out, diff_lambda, diff_subln,
           na_w_in, na_w_out, na_rpb,
           gqa_w_in, gqa_w_out, gqa_qk_norm):
    b, s, d = x.shape
    n = b * s
    h = x.reshape(n, d)

    for i in range(DEPTH):
        g = norm_g[i]
        h = _ffn(h, g[0:2], ffn1_wg[i].astype(BF16), ffn1_wu[i].astype(BF16), ffn1_wd[i].astype(BF16))
        kind, j = i % N_MIXERS, i // N_MIXERS
        h3 = h.reshape(b, s, d)
        if kind == 0:
            w = diff_w_in[j]
            wqT = w[:, 0:d].T.astype(BF16)
            wkT = w[:, d:2 * d].T.astype(BF16)
            wvT = w[:, 2 * d:3 * d].T.astype(BF16)
            ct, st = _diff_tables(s)
            qT, k, vT = _inproj_t(h3, g[2:3], wqT, wkT, wvT, ct, st, "diff")
            lam_init = 0.8 - 0.6 * math.exp(-0.3 * i)
            gsub_b = _col_gain(diff_subln[j], _FLASH_LANES["diff"] // 2)
            o = _flash(qT, k, vT, "diff", diff_lambda[j].astype(F32), gsub_b, lam_init)
            w_out = diff_w_out[j]
        elif kind == 1:
            q, k, v = _inproj_n(h3, g[2:3], na_w_in[j].astype(BF16))
            rpb = na_rpb[j].astype(F32)
            rpb_pad = jnp.zeros((rpb.shape[0], 16, 128), F32).at[:, :rpb.shape[1], :rpb.shape[2]].set(rpb)
            pt = _na_table(rpb_pad)
            o = _na_attention(q, k, v, pt)
            w_out = na_w_out[j]
        else:
            w = gqa_w_in[j]
            nq = d
            nkv = (w.shape[1] - nq) // 2
            wqT = w[:, 0:nq].T.astype(BF16)
            wkT = w[:, nq:nq + nkv].T.astype(BF16)
            wvT = w[:, nq + nkv:].T.astype(BF16)
            ct, st = _axial_tables(s)
            qk_gains = (_col_gain(gqa_qk_norm[j, 0], TOKEN_TILE), _col_gain(gqa_qk_norm[j, 1], TOKEN_TILE))
            qT, k, vT = _inproj_t(h3, g[2:3], wqT, wkT, wvT, ct, st, "gqa", qk_gains)
            o = _flash(qT, k, vT, "gqa")
            w_out = gqa_w_out[j]
        h = _outproj_ffn(o.reshape(n, d), h, g[3:6], w_out.astype(BF16),
                         ffn2_wg[i].astype(BF16), ffn2_wu[i].astype(BF16), ffn2_wd[i].astype(BF16))
    return h.reshape(b, s, d)
```

```python
import functools
import math

import jax
import jax.numpy as jnp
from jax import lax
from jax.experimental import pallas as pl
from jax.experimental.pallas import tpu as pltpu

D_MODEL = 1024
SEQ = 8192
DEPTH = 4
N_MIXERS = 3
GRID_W = 64
EPS = 1e-6
D_FF = 2816
FFN_RES = 0.5
HEAD_DIM = 64
ROPE_THETA = 500000.0
ROPE_DIMS = 16
NA_WIN_H = 8
NA_WIN_W = 16
AXIAL_THETA = 10000.0

LOG2E = 1.4426950408889634
QSCALE = (HEAD_DIM ** -0.5) * LOG2E
NEG = -1e30

VMEM_LIMIT_BYTES = 56 * 1024 * 1024
TOKEN_TILE = 512
FLASH_KEY_BLOCK = 512
FLASH_UNROLL = 8

BF16 = jnp.bfloat16
F32 = jnp.float32

_NT = (((1,), (1,)), ((), ()))


def _cparams(n_axes):
    return pltpu.CompilerParams(
        dimension_semantics=("parallel",) * n_axes,
        vmem_limit_bytes=VMEM_LIMIT_BYTES,
    )


def _resident(shape, index_map):
    return pl.BlockSpec(shape, index_map, pipeline_mode=pl.Buffered(1))


def _rms_rows(x, g):
    ms = jnp.mean(x * x, axis=-1, keepdims=True)
    return x * lax.rsqrt(ms + EPS) * g


def _ffn_block(x, g_pre, g_post, wg_ref, wu_ref, wd_ref):
    xn = _rms_rows(x, g_pre).astype(BF16)
    gate = jnp.dot(xn, wg_ref[...], preferred_element_type=F32)
    up = jnp.dot(xn, wu_ref[...], preferred_element_type=F32)
    act = (gate * jax.nn.sigmoid(gate) * up).astype(BF16)
    y = jnp.dot(act, wd_ref[...], preferred_element_type=F32)
    return x + FFN_RES * _rms_rows(y, g_post)


def _ffn_kernel(x_ref, g_ref, wg_ref, wu_ref, wd_ref, o_ref):
    g = g_ref[...]
    o_ref[...] = _ffn_block(x_ref[...], g[0:1], g[1:2], wg_ref, wu_ref, wd_ref)


def _outproj_ffn_kernel(o_ref, h_ref, g_ref, wo_ref, wg_ref, wu_ref, wd_ref, out_ref):
    g = g_ref[...]
    m = jnp.dot(o_ref[...], wo_ref[...], preferred_element_type=F32)
    h1 = h_ref[...] + _rms_rows(m, g[0:1])
    out_ref[...] = _ffn_block(h1, g[1:2], g[2:3], wg_ref, wu_ref, wd_ref)


def _outproj_ffn(o2d, h2d, g3, wo, wg, wu, wd, tm=TOKEN_TILE):
    n = h2d.shape[0]
    blk = pl.BlockSpec((tm, D_MODEL), lambda i: (i, 0))
    return pl.pallas_call(
        _outproj_ffn_kernel,
        grid=(n // tm,),
        in_specs=[
            blk, blk,
            _resident((3, D_MODEL), lambda i: (0, 0)),
            _resident((D_MODEL, D_MODEL), lambda i: (0, 0)),
            _resident((D_MODEL, D_FF), lambda i: (0, 0)),
            _resident((D_MODEL, D_FF), lambda i: (0, 0)),
            _resident((D_FF, D_MODEL), lambda i: (0, 0)),
        ],
        out_specs=blk,
        out_shape=jax.ShapeDtypeStruct(h2d.shape, F32),
        compiler_params=_cparams(1),
        name="outproj_ffn",
    )(o2d, h2d, g3, wo, wg, wu, wd)


def _ffn(h2d, g2, wg, wu, wd, tm=TOKEN_TILE):
    n = h2d.shape[0]
    return pl.pallas_call(
        _ffn_kernel,
        grid=(n // tm,),
        in_specs=[
            pl.BlockSpec((tm, D_MODEL), lambda i: (i, 0)),
            _resident((2, D_MODEL), lambda i: (0, 0)),
            _resident((D_MODEL, D_FF), lambda i: (0, 0)),
            _resident((D_MODEL, D_FF), lambda i: (0, 0)),
            _resident((D_FF, D_MODEL), lambda i: (0, 0)),
        ],
        out_specs=pl.BlockSpec((tm, D_MODEL), lambda i: (i, 0)),
        out_shape=jax.ShapeDtypeStruct(h2d.shape, F32),
        compiler_params=_cparams(1),
        name="ffn",
    )(h2d, g2, wg, wu, wd)


def _swap_halves(x, half):
    blocks = []
    r = 0
    n = x.shape[0]
    while r + 2 * half <= n:
        blocks += [x[r + half:r + 2 * half], x[r:r + half]]
        r += 2 * half
    return jnp.concatenate(blocks, axis=0)


def _inproj_t_kernel(h_ref, g_ref, wqT_ref, wkT_ref, wvT_ref, ct_ref, st_ref, *rest, variant):
    if variant == "gqa":
        gq_ref, gk_ref, qT_ref, k_ref, vT_ref = rest
        gq, gk = gq_ref[...], gk_ref[...]
    else:
        qT_ref, k_ref, vT_ref = rest
        gq = gk = None
    u = _rms_rows(h_ref[0], g_ref[...]).astype(BF16)
    ct = ct_ref[...]
    st = st_ref[...]

    def head(x, gain, scale):
        if variant == "gqa":
            ms = jnp.mean(x * x, axis=0, keepdims=True)
            x = x * lax.rsqrt(ms + EPS) * gain
            xs = _swap_halves(x, 16)
        else:
            xs = jnp.concatenate([x[8:16], x[0:8], x[16:64]], axis=0)
        x = x * ct + xs * st
        if scale is not None:
            x = x * scale
        return x

    qT = lax.dot_general(wqT_ref[...], u, _NT, preferred_element_type=F32)
    for hh in range(qT.shape[0] // HEAD_DIM):
        sl = slice(hh * HEAD_DIM, (hh + 1) * HEAD_DIM)
        qT_ref[0, sl, :] = head(qT[sl], gq, QSCALE).astype(BF16)

    kT = lax.dot_general(wkT_ref[...], u, _NT, preferred_element_type=F32)
    for pp in range(kT.shape[0] // 128):
        pair = jnp.concatenate(
            [head(kT[pp * 128 + e * 64: pp * 128 + (e + 1) * 64], gk, None) for e in range(2)], axis=0)
        k_ref[0, :, pp * 128:(pp + 1) * 128] = pair.T.astype(BF16)

    vT = lax.dot_general(wvT_ref[...], u, _NT, preferred_element_type=F32)
    vT_ref[0] = vT.astype(BF16)


def _inproj_t(h3d, g, wqT, wkT, wvT, ct, st, variant, qk_gains=(), tm=TOKEN_TILE):
    b, s, _ = h3d.shape
    nq, nk, nv = wqT.shape[0], wkT.shape[0], wvT.shape[0]
    assert len(qk_gains) == (2 if variant == "gqa" else 0)
    kern = functools.partial(_inproj_t_kernel, variant=variant)
    return pl.pallas_call(
        kern,
        grid=(b, s // tm),
        in_specs=[
            pl.BlockSpec((1, tm, D_MODEL), lambda bi, ti: (bi, ti, 0)),
            _resident((1, D_MODEL), lambda bi, ti: (0, 0)),
            _resident((nq, D_MODEL), lambda bi, ti: (0, 0)),
            _resident((nk, D_MODEL), lambda bi, ti: (0, 0)),
            _resident((nv, D_MODEL), lambda bi, ti: (0, 0)),
            pl.BlockSpec((HEAD_DIM, tm), lambda bi, ti: (0, ti)),
            pl.BlockSpec((HEAD_DIM, tm), lambda bi, ti: (0, ti)),
        ] + [_resident((HEAD_DIM, tm), lambda bi, ti: (0, 0)) for _ in qk_gains],
        out_specs=[
            pl.BlockSpec((1, nq, tm), lambda bi, ti: (bi, 0, ti)),
            pl.BlockSpec((1, tm, nk), lambda bi, ti: (bi, ti, 0)),
            pl.BlockSpec((1, nv, tm), lambda bi, ti: (bi, 0, ti)),
        ],
        out_shape=[
            jax.ShapeDtypeStruct((b, nq, s), BF16),
            jax.ShapeDtypeStruct((b, s, nk), BF16),
            jax.ShapeDtypeStruct((b, nv, s), BF16),
        ],
        compiler_params=_cparams(2),
        name="inproj_" + variant,
    )(h3d, g, wqT, wkT, wvT, ct, st, *qk_gains)


def _inproj_n_kernel(h_ref, g_ref, w_ref, q_ref, k_ref, v_ref):
    u = _rms_rows(h_ref[0], g_ref[...]).astype(BF16)
    y = jnp.dot(u, w_ref[...], preferred_element_type=F32)
    q_ref[0] = (y[:, 0:D_MODEL] * QSCALE).astype(BF16)
    k_ref[0] = y[:, D_MODEL:2 * D_MODEL].astype(BF16)
    v_ref[0] = y[:, 2 * D_MODEL:3 * D_MODEL].astype(BF16)


def _inproj_n(h3d, g, w, tm=TOKEN_TILE):
    b, s, _ = h3d.shape
    blk = pl.BlockSpec((1, tm, D_MODEL), lambda bi, ti: (bi, ti, 0))
    return pl.pallas_call(
        _inproj_n_kernel,
        grid=(b, s // tm),
        in_specs=[
            blk,
            _resident((1, D_MODEL), lambda bi, ti: (0, 0)),
            _resident((D_MODEL, 3 * D_MODEL), lambda bi, ti: (0, 0)),
        ],
        out_specs=[blk, blk, blk],
        out_shape=[jax.ShapeDtypeStruct((b, s, D_MODEL), BF16)] * 3,
        compiler_params=_cparams(2),
        name="inproj_na",
    )(h3d, g, w)


_SUM_ROWS = 16
_Q_UNROLL = 2
_MIN_PV_ROWS = 128
_RING = 4
_LOOKAHEAD = 3
_PV_LAG = 1
_P_RING = 2
_FLASH_LANES = {"diff": 1024, "gqa": 1024}


def _flash_kernel(qT_ref, k_ref, vT_ref, *rest, variant, qb, nq, kb, nk, unroll, lam_init):
    if variant == "diff":
        lam_ref, gsub_ref, o_ref, s_ring, p_ring, acc_sc = rest
    else:
        o_ref, s_ring, p_ring, acc_sc = rest

    def padded_queries(qq):
        qT = qT_ref[0, :, pl.ds(pl.multiple_of(qq * qb, qb), qb)]
        if variant == "diff":
            zero = jnp.zeros((HEAD_DIM, qb), BF16)
            return jnp.concatenate(
                [jnp.concatenate([qT[0:64], zero], axis=0),
                 jnp.concatenate([zero, qT[64:128]], axis=0)], axis=1)
        parity = pl.program_id(1) % 2
        row_half = (lax.broadcasted_iota(jnp.int32, (128, qb), 0) >= HEAD_DIM).astype(jnp.int32)
        keep = row_half == parity
        parts = []
        for g in range(4):
            qg = qT[g * 64:(g + 1) * 64]
            parts.append(jnp.where(keep, jnp.concatenate([qg, qg], axis=0), jnp.zeros((), BF16)))
        return jnp.concatenate(parts, axis=1)

    lanes = s_ring.shape[2]
    dv = vT_ref.shape[1]
    ones_rows = jnp.ones((max(_SUM_ROWS, _MIN_PV_ROWS - dv), kb), BF16)

    def scores(qz, i):
        ks = i * kb if isinstance(i, int) else pl.multiple_of(i * kb, kb)
        return jnp.dot(k_ref[0, pl.ds(ks, kb), :], qz, preferred_element_type=F32)

    def weighted_values(i, p):
        ks = pl.multiple_of(i * kb, kb)
        v_ext = jnp.concatenate([vT_ref[0, :, pl.ds(ks, kb)], ones_rows], axis=0)
        return jnp.dot(v_ext, p, preferred_element_type=F32)[0:dv + _SUM_ROWS]

    def first_scores(qq):
        qz = padded_queries(qq)
        first = [scores(qz, t) for t in range(_LOOKAHEAD)]
        for t in range(_LOOKAHEAD):
            s_ring[t] = first[t]
        return tuple(jnp.max(f, axis=0, keepdims=True) for f in first)

    def stage(qz, i, slot, stats, score_qz=None, score_block=None):
        alphas, m_prev, maxes = stats
        m_new = jnp.maximum(m_prev, maxes[0])
        s_new = scores(qz if score_qz is None else score_qz,
                       i + _LOOKAHEAD if score_block is None else score_block)
        s_ring[(slot + _LOOKAHEAD) % _RING] = s_new
        maxes = maxes[1:] + (jnp.max(s_new, axis=0, keepdims=True),)
        acc_sc[...] = alphas[0] * acc_sc[...] + weighted_values(
            jnp.maximum(i - _PV_LAG, 0), p_ring[(slot - _PV_LAG) % _P_RING])
        p_ring[slot % _P_RING] = jnp.exp2(s_ring[slot % _RING] - m_new).astype(BF16)
        return alphas[1:] + (jnp.exp2(m_prev - m_new),), m_new, maxes

    n_body = nk // unroll

    def query_block(qq, maxes):
        qz = padded_queries(qq)
        for t in range(1, _PV_LAG + 1):
            p_ring[-t % _P_RING] = jnp.zeros(p_ring.shape[1:], BF16)
        acc_sc[...] = jnp.zeros(acc_sc.shape, F32)
        stats = ((jnp.ones((1, lanes), F32),) * _PV_LAG, jnp.full((1, lanes), NEG, F32), maxes)

        def body(ii, stats):
            for u in range(unroll):
                stats = stage(qz, unroll * ii + u, u, stats)
            return stats

        stats = lax.fori_loop(0, n_body - 1, body, stats)
        qz_next = padded_queries(jnp.minimum(qq + 1, nq - 1))
        for u in range(unroll):
            i_last = unroll * (n_body - 1) + u
            ahead = i_last + _LOOKAHEAD
            if ahead < nk:
                stats = stage(qz, i_last, u, stats)
            else:
                stats = stage(qz, i_last, u, stats, score_qz=qz_next, score_block=ahead - nk)
        acc = acc_sc[...]
        for t in range(_PV_LAG):
            blk = nk - _PV_LAG + t
            acc = stats[0][t] * acc + weighted_values(blk, p_ring[blk % _P_RING])

        o = acc[0:dv] / acc[dv:dv + 1]
        rows = pl.ds(pl.multiple_of(qq * qb, qb), qb)
        if variant == "diff":
            lv = lam_ref[...]
            lam = (jnp.exp(jnp.sum(lv[0:1] * lv[1:2], axis=1, keepdims=True))
                   - jnp.exp(jnp.sum(lv[2:3] * lv[3:4], axis=1, keepdims=True)) + lam_init)
            od = o[:, 0:qb] - lam * o[:, qb:2 * qb]
            ms = jnp.mean(od * od, axis=0, keepdims=True)
            od = od * lax.rsqrt(ms + EPS) * gsub_ref[...] * (1.0 - lam_init)
            o_ref[0, rows, :] = od.T.astype(BF16)
        else:
            stacked = jnp.concatenate([o[:, g * qb:(g + 1) * qb] for g in range(4)], axis=0)
            o_ref[0, rows, :] = stacked.T.astype(BF16)
        return stats[2]

    assert nk % _RING == 0 and _LOOKAHEAD <= unroll and nq % _Q_UNROLL == 0

    def query_blocks(qg, maxes):
        for t in range(_Q_UNROLL):
            maxes = query_block(_Q_UNROLL * qg + t, maxes)
        return maxes

    lax.fori_loop(0, nq // _Q_UNROLL, query_blocks, first_scores(0))


def _flash(qT, k, vT, variant, lam_vecs=None, gsub_b=None, lam_init=0.0,
           kb=FLASH_KEY_BLOCK, unroll=FLASH_UNROLL):
    b, _, s = qT.shape
    lanes = _FLASH_LANES[variant]
    if variant == "diff":
        qb, n_groups, r, dv, width = lanes // 2, 8, 128, 128, 128
        k_map = lambda bi, j: (bi, 0, j)
    else:
        qb, n_groups, r, dv, width = lanes // 4, 4, 256, 64, 256
        k_map = lambda bi, j: (bi, 0, j // 2)
    nk = s // kb
    assert unroll % _RING == 0 and nk % unroll == 0
    kern = functools.partial(_flash_kernel, variant=variant, qb=qb, nq=s // qb, kb=kb, nk=nk,
                             unroll=unroll, lam_init=lam_init)
    in_specs = [
        pl.BlockSpec((1, r, s), lambda bi, j: (bi, j, 0)),
        pl.BlockSpec((1, s, 128), k_map),
        pl.BlockSpec((1, dv, s), lambda bi, j: (bi, j, 0)),
    ]
    args = [qT, k, vT]
    if variant == "diff":
        in_specs += [
            _resident((4, HEAD_DIM), lambda bi, j: (0, 0)),
            _resident((128, qb), lambda bi, j: (0, 0)),
        ]
        args += [lam_vecs, gsub_b]
    return pl.pallas_call(
        kern,
        grid=(b, n_groups),
        in_specs=in_specs,
        out_specs=pl.BlockSpec((1, s, width), lambda bi, j: (bi, 0, j)),
        out_shape=jax.ShapeDtypeStruct((b, s, D_MODEL), BF16),
        scratch_shapes=[
            pltpu.VMEM((_RING, kb, lanes), F32),
            pltpu.VMEM((_P_RING, kb, lanes), BF16),
            pltpu.VMEM((dv + _SUM_ROWS, lanes), F32),
        ],
        compiler_params=_cparams(2),
        name="flash_" + variant,
    )(*args)


def _na_table_kernel(rpb_ref, pt_ref):
    cq = lax.broadcasted_iota(jnp.int32, (GRID_W, 128), 0)
    ck = lax.broadcasted_iota(jnp.int32, (GRID_W, 128), 1) & (GRID_W - 1)
    c0 = jnp.clip(cq - NA_WIN_W // 2, 0, GRID_W - NA_WIN_W)
    valid = (ck >= c0) & (ck < c0 + NA_WIN_W)
    for dr in range(2 * NA_WIN_H - 2):
        a = rpb_ref[0, dr:dr + 1, :]
        bb = rpb_ref[0, dr + 1:dr + 2, :]
        x = a + pltpu.roll(bb, GRID_W, axis=1)
        xb = jnp.broadcast_to(x, (GRID_W, 128))
        t = pltpu.roll(xb, 128 - (NA_WIN_W - 1), axis=1, stride=1, stride_axis=0)
        pt_ref[0, dr] = jnp.where(valid, t * LOG2E, NEG)


def _na_table(rpb_pad):
    nh = rpb_pad.shape[0]
    ndr = 2 * NA_WIN_H - 2
    return pl.pallas_call(
        _na_table_kernel,
        grid=(nh,),
        in_specs=[pl.BlockSpec((1, 16, 128), lambda h: (h, 0, 0))],
        out_specs=pl.BlockSpec((1, ndr, GRID_W, 128), lambda h: (h, 0, 0, 0)),
        out_shape=jax.ShapeDtypeStruct((nh, ndr, GRID_W, 128), F32),
        compiler_params=_cparams(1),
        name="na_table",
    )(rpb_pad)


_NA_ROWS = 4
_NA_BAND = _NA_ROWS + NA_WIN_H


def _na_kernel(q_ref, k0_ref, k1_ref, k2_ref, v0_ref, v1_ref, v2_ref, pt_ref, o_ref, kband, vband,
               *, n_steps):
    i = pl.program_id(1)
    blk = _NA_ROWS * GRID_W
    for n, (kr, vr) in enumerate(((k0_ref, v0_ref), (k1_ref, v1_ref), (k2_ref, v2_ref))):
        kband[n * blk:(n + 1) * blk, :] = kr[0]
        vband[n * blk:(n + 1) * blk, :] = vr[0]
    left = lax.broadcasted_iota(jnp.int32, (GRID_W, 128), 1) < HEAD_DIM
    nkeys = NA_WIN_H * GRID_W

    def row_body(t, carry):
        off = jnp.where(i == 0, 0, jnp.where(i == n_steps - 1, _NA_ROWS, t))
        sidx = jnp.where(i == 0, NA_WIN_H - 1 - t, jnp.where(i == n_steps - 1, 3 - t, 3))
        start = pl.multiple_of(off * GRID_W, GRID_W)
        qs = pl.multiple_of(t * GRID_W, GRID_W)
        n_pairs = D_MODEL // 128
        cols = [slice(j * 128, (j + 1) * 128) for j in range(n_pairs)]
        scores = []
        for j in range(n_pairs):
            qp = q_ref[0, pl.ds(qs, GRID_W), cols[j]]
            kp = kband[pl.ds(start, nkeys), cols[j]]
            qz = jnp.concatenate([jnp.where(left, qp, jnp.zeros((), BF16)),
                                  jnp.where(left, jnp.zeros((), BF16), qp)], axis=0)
            scores.append(lax.dot_general(qz, kp, _NT, preferred_element_type=F32))
        probs, denoms = [], []
        for j in range(n_pairs):
            bias = jnp.concatenate(
                [jnp.concatenate([pt_ref[2 * j + e, sidx + 2 * m] for m in range(NA_WIN_H // 2)], axis=1)
                 for e in range(2)], axis=0)
            s = scores[j] + bias
            p = jnp.exp2(s - jnp.max(s, axis=1, keepdims=True))
            denoms.append(jnp.sum(p, axis=1, keepdims=True))
            probs.append(p.astype(BF16))
        for j in range(n_pairs):
            vp = vband[pl.ds(start, nkeys), cols[j]]
            o = jnp.dot(probs[j], vp, preferred_element_type=F32) / denoms[j]
            o_ref[0, pl.ds(qs, GRID_W), cols[j]] = jnp.where(
                left, o[0:GRID_W], o[GRID_W:2 * GRID_W]).astype(BF16)
        return carry

    lax.fori_loop(0, _NA_ROWS, row_body, 0)


def _na_attention(q, k, v, pt):
    b, s, _ = q.shape
    blk = _NA_ROWS * GRID_W
    n_steps = s // blk
    last = n_steps - 3

    def band_spec(n):
        return pl.BlockSpec((1, blk, D_MODEL),
                            lambda bi, i: (bi, jnp.clip(i - 1, 0, last) + n, 0))

    kern = functools.partial(_na_kernel, n_steps=n_steps)
    return pl.pallas_call(
        kern,
        grid=(b, n_steps),
        in_specs=[
            pl.BlockSpec((1, blk, D_MODEL), lambda bi, i: (bi, i, 0)),
            band_spec(0), band_spec(1), band_spec(2),
            band_spec(0), band_spec(1), band_spec(2),
            _resident(pt.shape, lambda bi, i: (0, 0, 0, 0)),
        ],
        out_specs=pl.BlockSpec((1, blk, D_MODEL), lambda bi, i: (bi, i, 0)),
        out_shape=jax.ShapeDtypeStruct((b, s, D_MODEL), BF16),
        scratch_shapes=[
            pltpu.VMEM((_NA_BAND * GRID_W, D_MODEL), BF16),
            pltpu.VMEM((_NA_BAND * GRID_W, D_MODEL), BF16),
        ],
        compiler_params=_cparams(2),
        name="na_attention",
    )(q, k, k, k, v, v, v, pt)


def _angles(pos, dims, theta):
    inv = theta ** (-jnp.arange(0, dims, 2, dtype=F32) / dims)
    return pos[:, None] * inv[None, :]


def _diff_tables(s):
    ang = _angles(jnp.arange(s, dtype=F32), ROPE_DIMS, ROPE_THETA)
    c, sn = jnp.cos(ang).T, jnp.sin(ang).T
    ones = jnp.ones((HEAD_DIM - ROPE_DIMS, s), F32)
    ct = jnp.concatenate([c, c, ones], axis=0)
    st = jnp.concatenate([-sn, sn, 0.0 * ones], axis=0)
    return ct, st


def _axial_tables(s):
    t = jnp.arange(s)
    half = HEAD_DIM // 2
    ar = _angles((t // GRID_W).astype(F32), half, AXIAL_THETA)
    ac = _angles((t % GRID_W).astype(F32), half, AXIAL_THETA)
    cr, sr, cc, sc = jnp.cos(ar).T, jnp.sin(ar).T, jnp.cos(ac).T, jnp.sin(ac).T
    ct = jnp.concatenate([cr, cr, cc, cc], axis=0)
    st = jnp.concatenate([-sr, sr, -sc, sc], axis=0)
    return ct, st


def _col_gain(g, tm):
    return jnp.broadcast_to(g.astype(F32)[:, None], (g.shape[0], tm))


def kernel(x, norm_g, ffn1_wg, ffn1_wu, ffn1_wd, ffn2_wg, ffn2_wu, ffn2_wd,
           diff_w_in, diff_w_out, diff_lambda, diff_subln,
           na_w_in, na_w_out, na_rpb,
           gqa_w_in, gqa_w_out, gqa_qk_norm):
    b, s, d = x.shape
    n = b * s
    h = x.reshape(n, d)

    for i in range(DEPTH):
        g = norm_g[i]
        h = _ffn(h, g[0:2], ffn1_wg[i].astype(BF16), ffn1_wu[i].astype(BF16), ffn1_wd[i].astype(BF16))
        kind, j = i % N_MIXERS, i // N_MIXERS
        h3 = h.reshape(b, s, d)
        if kind == 0:
            w = diff_w_in[j]
            wqT = w[:, 0:d].T.astype(BF16)
            wkT = w[:, d:2 * d].T.astype(BF16)
            wvT = w[:, 2 * d:3 * d].T.astype(BF16)
            ct, st = _diff_tables(s)
            qT, k, vT = _inproj_t(h3, g[2:3], wqT, wkT, wvT, ct, st, "diff")
            lam_init = 0.8 - 0.6 * math.exp(-0.3 * i)
            gsub_b = _col_gain(diff_subln[j], _FLASH_LANES["diff"] // 2)
            o = _flash(qT, k, vT, "diff", diff_lambda[j].astype(F32), gsub_b, lam_init)
            w_out = diff_w_out[j]
        elif kind == 1:
            q, k, v = _inproj_n(h3, g[2:3], na_w_in[j].astype(BF16))
            rpb = na_rpb[j].astype(F32)
            rpb_pad = jnp.zeros((rpb.shape[0], 16, 128), F32).at[:, :rpb.shape[1], :rpb.shape[2]].set(rpb)
            pt = _na_table(rpb_pad)
            o = _na_attention(q, k, v, pt)
            w_out = na_w_out[j]
        else:
            w = gqa_w_in[j]
            nq = d
            nkv = (w.shape[1] - nq) // 2
            wqT = w[:, 0:nq].T.astype(BF16)
            wkT = w[:, nq:nq + nkv].T.astype(BF16)
            wvT = w[:, nq + nkv:].T.astype(BF16)
            ct, st = _axial_tables(s)
            qk_gains = (_col_gain(gqa_qk_norm[j, 0], TOKEN_TILE), _col_gain(gqa_qk_norm[j, 1], TOKEN_TILE))
            qT, k, vT = _inproj_t(h3, g[2:3], wqT, wkT, wvT, ct, st, "gqa", qk_gains)
            o = _flash(qT, k, vT, "gqa")
            w_out = gqa_w_out[j]
        h = _outproj_ffn(o.reshape(n, d), h, g[3:6], w_out.astype(BF16),
                         ffn2_wg[i].astype(BF16), ffn2_wu[i].astype(BF16), ffn2_wd[i].astype(BF16))
    return h.reshape(b, s, d)
```
